```python
import math
import jax, jax.numpy as jnp
from jax import lax
import numpy as np

D_MODEL = 1024
BATCH = 4
SEQ = 4096
DEPTH = 2
DEC_BATCH = 32
DEC_SEQ = 4
PAST_LEN = 16384
PAGE_SIZE = 128

N_HEADS_ATT = 4
HEAD_DIM = 64
V_DIM = 2 * HEAD_DIM
ATT_WIDTH = N_HEADS_ATT * V_DIM
QK_WIDTH = N_HEADS_ATT * 2 * HEAD_DIM
ROT_DIM = HEAD_DIM // 4
ROPE_THETA = 500000.0
ATT_SCALE = HEAD_DIM ** -0.5
Q_BLOCK = 128
POOL_WINDOWS = (2, 4, 8, 16)
N_POOL_GROUPS = len(POOL_WINDOWS)
POOL_WIDTH = D_MODEL - ATT_WIDTH
POOL_GROUP = POOL_WIDTH // N_POOL_GROUPS
POOL_HIST = max(POOL_WINDOWS) - 1
IN_WIDTH = 2 * QK_WIDTH + ATT_WIDTH + POOL_WIDTH
D_FF = ((((8 * D_MODEL) + 2) // 3 + 255) // 256) * 256
PLE_DIM = 256
EPS = 1e-6

kernel_name = "hybrid_diffattn_pool_decoder_step"


def rmsnorm(x, g):
    xf = x.astype(jnp.float32)
    y = xf * lax.rsqrt(jnp.mean(xf * xf, axis=-1, keepdims=True) + EPS)
    return (y * g.astype(jnp.float32)).astype(x.dtype)


def rope(x, pos):
    half = ROT_DIM // 2
    inv = jnp.power(ROPE_THETA, -jnp.arange(0, ROT_DIM, 2, dtype=jnp.float32) / ROT_DIM)
    ang = pos.astype(jnp.float32)[:, None] * inv[None, :]
    cos = jnp.cos(ang)[None, :, None, None, :]
    sin = jnp.sin(ang)[None, :, None, None, :]
    xr = x[..., :ROT_DIM].astype(jnp.float32)
    x1, x2 = xr[..., :half], xr[..., half:]
    rot = jnp.concatenate([x1 * cos - x2 * sin, x2 * cos + x1 * sin], axis=-1).astype(x.dtype)
    return jnp.concatenate([rot, x[..., ROT_DIM:]], axis=-1)


def project(h, w_in, pos):
    b, T, _ = h.shape
    z = h @ w_in
    q = z[..., :QK_WIDTH].reshape(b, T, N_HEADS_ATT, 2, HEAD_DIM)
    k = z[..., QK_WIDTH:2 * QK_WIDTH].reshape(b, T, N_HEADS_ATT, 2, HEAD_DIM)
    v = z[..., 2 * QK_WIDTH:2 * QK_WIDTH + ATT_WIDTH].reshape(b, T, N_HEADS_ATT, V_DIM)
    u = z[..., 2 * QK_WIDTH + ATT_WIDTH:]
    return rope(q, pos), rope(k, pos), v, u


def diff_attn_core(q, k, v, q_pos, k_pos, lam):
    s = jnp.einsum('bqhcd,bkhcd->bhcqk', q, k).astype(jnp.float32) * ATT_SCALE
    mask = q_pos[:, None] >= k_pos[None, :]
    s = jnp.where(mask[None, None, None], s, -1e30)
    p = jax.nn.softmax(s, axis=-1)
    a = p[:, :, 0] - lam * p[:, :, 1]
    return jnp.einsum('bhqk,bkhe->bqhe', a.astype(v.dtype), v)


def prompt_attn(q, k, v, lam):
    b, S = q.shape[:2]
    nb = S // Q_BLOCK
    qb = q.reshape(b, nb, Q_BLOCK, N_HEADS_ATT, 2, HEAD_DIM).transpose(1, 0, 2, 3, 4, 5)
    k_pos = jnp.arange(S)

    def block(args):
        qi, i = args
        q_pos = i * Q_BLOCK + jnp.arange(Q_BLOCK)
        return diff_attn_core(qi, k, v, q_pos, k_pos, lam)

    o = lax.map(block, (qb, jnp.arange(nb)))
    return o.transpose(1, 0, 2, 3, 4).reshape(b, S, N_HEADS_ATT, V_DIM)


def sample_attn(q, k_new, v_new, cache_k, cache_v, layer, page_table, lam):
    T = q.shape[1]
    q_pos = PAST_LEN + jnp.arange(T)
    k_pos = jnp.arange(PAST_LEN + T)

    def one_seq(args):
        qi, ki, vi, pt = args
        kp = cache_k[layer, pt].reshape(PAST_LEN, N_HEADS_ATT, 2, HEAD_DIM)
        vp = cache_v[layer, pt].reshape(PAST_LEN, N_HEADS_ATT, V_DIM)
        k_all = jnp.concatenate([kp.astype(ki.dtype), ki], axis=0)
        v_all = jnp.concatenate([vp.astype(vi.dtype), vi], axis=0)
        return diff_attn_core(qi[None], k_all[None], v_all[None], q_pos, k_pos, lam)[0]

    return lax.map(one_seq, (q, k_new, v_new, page_table))


def head_out(o, g_sub, lam_init):
    b, T = o.shape[:2]
    of = o.astype(jnp.float32)
    y = of * lax.rsqrt(jnp.mean(of * of, axis=-1, keepdims=True) + EPS)
    y = y * g_sub.astype(jnp.float32) * (1.0 - lam_init)
    return y.astype(o.dtype).reshape(b, T, ATT_WIDTH)


def pool_mix(u_ext, pos, w_pool, pool_scale):
    b, L, C = u_ext.shape
    T = L - POOL_HIST
    P = POOL_HIST
    cs0 = jnp.concatenate([jnp.zeros((b, 1, C), jnp.float32),
                           jnp.cumsum(u_ext.astype(jnp.float32), axis=1)], axis=1)
    end = cs0[:, P + 1:P + 1 + T]
    means = []
    for g, w in enumerate(POOL_WINDOWS):
        sl = slice(g * POOL_GROUP, (g + 1) * POOL_GROUP)
        start = cs0[:, P + 1 - w:P + 1 - w + T, sl]
        cnt = jnp.minimum(pos + 1, w).astype(jnp.float32)[None, :, None]
        means.append((end[..., sl] - start) / cnt)
    pooled = jnp.concatenate(means, axis=-1) - u_ext[:, P:].astype(jnp.float32)
    pooled = pooled.astype(u_ext.dtype).reshape(b, T, N_POOL_GROUPS, POOL_GROUP)
    y = jnp.einsum('btgc,gcd->btgd', pooled, w_pool).reshape(b, T, POOL_WIDTH)
    return y * pool_scale


def finish_layer(x, att, pool, p, w_o, g_post_mix, g_pre_ffn, g_post_ffn,
                 w_gate, w_up, w_down, w_ple_gate, w_ple_proj):
    mix = jnp.concatenate([att, pool], axis=-1) @ w_o
    x = x + rmsnorm(mix, g_post_mix)
    h = rmsnorm(x, g_pre_ffn)
    f = (jax.nn.silu(h @ w_gate) * (h @ w_up)) @ w_down
    x = x + rmsnorm(f, g_post_ffn)
    x = x + jax.nn.sigmoid(x @ w_ple_gate) * (p @ w_ple_proj)
    return x


def setup_inputs(seed: int = 0) -> dict:
    key = jax.random.key(seed)
    ks = jax.random.split(key, 24)
    f32 = jnp.float32
    n_pages = PAST_LEN // PAGE_SIZE
    n_used = DEC_BATCH * n_pages
    n_pool = (5 * n_used + 3) // 4
    nrm = lambda k, s, sc: jax.random.normal(k, s, f32) * sc
    perm = jax.random.permutation(ks[0], n_pool)[:n_used]
    return {
        "x_prompt": nrm(ks[1], (BATCH, SEQ, D_MODEL), 1.0),
        "x_sample": nrm(ks[2], (DEC_BATCH, DEC_SEQ, D_MODEL), 1.0),
        "p_prompt": nrm(ks[3], (DEPTH, BATCH, SEQ, PLE_DIM), 1.0),
        "p_sample": nrm(ks[4], (DEPTH, DEC_BATCH, DEC_SEQ, PLE_DIM), 1.0),
        "cache_k": nrm(ks[5], (DEPTH, n_pool, PAGE_SIZE, N_HEADS_ATT, 2 * HEAD_DIM), 1.0),
        "cache_v": nrm(ks[6], (DEPTH, n_pool, PAGE_SIZE, N_HEADS_ATT, V_DIM), 1.0),
        "state_pool": nrm(ks[7], (DEPTH, DEC_BATCH, POOL_HIST, POOL_WIDTH), 1.0),
        "page_table": perm.reshape(DEC_BATCH, n_pages).astype(jnp.int32),
        "w_in": nrm(ks[8], (DEPTH, D_MODEL, IN_WIDTH), D_MODEL ** -0.5),
        "lambda_params": nrm(ks[9], (DEPTH, 4, HEAD_DIM), 0.1),
        "g_sub": 1.0 + nrm(ks[10], (DEPTH, V_DIM), 0.1),
        "w_pool": nrm(ks[11], (DEPTH, N_POOL_GROUPS, POOL_GROUP, POOL_GROUP), POOL_GROUP ** -0.5),
        "pool_scale": 1.0 + nrm(ks[12], (DEPTH, POOL_WIDTH), 0.1),
        "w_o": nrm(ks[13], (DEPTH, D_MODEL, D_MODEL), D_MODEL ** -0.5),
        "g_pre_mix": 1.0 + nrm(ks[14], (DEPTH, D_MODEL), 0.1),
        "g_post_mix": 1.0 + nrm(ks[15], (DEPTH, D_MODEL), 0.1),
        "g_pre_ffn": 1.0 + nrm(ks[16], (DEPTH, D_MODEL), 0.1),
        "g_post_ffn": 1.0 + nrm(ks[17], (DEPTH, D_MODEL), 0.1),
        "w_gate": nrm(ks[18], (DEPTH, D_MODEL, D_FF), D_MODEL ** -0.5),
        "w_up": nrm(ks[19], (DEPTH, D_MODEL, D_FF), D_MODEL ** -0.5),
        "w_down": nrm(ks[20], (DEPTH, D_FF, D_MODEL), D_FF ** -0.5),
        "w_ple_gate": nrm(ks[21], (DEPTH, D_MODEL, D_MODEL), D_MODEL ** -0.5),
        "w_ple_proj": nrm(ks[22], (DEPTH, PLE_DIM, D_MODEL), PLE_DIM ** -0.5),
    }


def reference(x_prompt, x_sample, p_prompt, p_sample, cache_k, cache_v, state_pool, page_table,
              w_in, lambda_params, g_sub, w_pool, pool_scale, w_o, g_pre_mix, g_post_mix,
              g_pre_ffn, g_post_ffn, w_gate, w_up, w_down, w_ple_gate, w_ple_proj):
    B, S, _ = x_prompt.shape
    Bd, T, _ = x_sample.shape
    pos_p = jnp.arange(S)
    pos_s = PAST_LEN + jnp.arange(T)
    xp, xs = x_prompt, x_sample
    kp_rows, vp_rows, pp_rows, ks_rows, vs_rows, ps_rows = [], [], [], [], [], []
    for i in range(DEPTH):
        lam_init = 0.8 - 0.6 * math.exp(-0.3 * i)
        lp = lambda_params[i].astype(jnp.float32)
        lam = jnp.exp(jnp.sum(lp[0] * lp[1])) - jnp.exp(jnp.sum(lp[2] * lp[3])) + lam_init
        ffn_w = (w_o[i], g_post_mix[i], g_pre_ffn[i], g_post_ffn[i], w_gate[i], w_up[i],
                 w_down[i], w_ple_gate[i], w_ple_proj[i])
        h = rmsnorm(xp, g_pre_mix[i])
        q, k, v, u = project(h, w_in[i], pos_p)
        att = head_out(prompt_attn(q, k, v, lam), g_sub[i], lam_init)
        u_ext = jnp.concatenate([jnp.zeros((B, POOL_HIST, POOL_WIDTH), u.dtype), u], axis=1)
        pool = pool_mix(u_ext, pos_p, w_pool[i], pool_scale[i])
        xp = finish_layer(xp, att, pool, p_prompt[i], *ffn_w)
        kp_rows.append(k.reshape(B, S, N_HEADS_ATT, 2 * HEAD_DIM))
        vp_rows.append(v)
        pp_rows.append(u_ext[:, -POOL_HIST:])
        h = rmsnorm(xs, g_pre_mix[i])
        q, k, v, u = project(h, w_in[i], pos_s)
        att = head_out(sample_attn(q, k, v, cache_k, cache_v, i, page_table, lam), g_sub[i], lam_init)
        u_ext = jnp.concatenate([state_pool[i].astype(u.dtype), u], axis=1)
        pool = pool_mix(u_ext, pos_s, w_pool[i], pool_scale[i])
        xs = finish_layer(xs, att, pool, p_sample[i], *ffn_w)
        ks_rows.append(k.reshape(Bd, T, N_HEADS_ATT, 2 * HEAD_DIM))
        vs_rows.append(v)
        ps_rows.append(u_ext[:, -POOL_HIST:])
    return (xp, xs, jnp.stack(kp_rows), jnp.stack(vp_rows), jnp.stack(pp_rows),
            jnp.stack(ks_rows), jnp.stack(vs_rows), jnp.stack(ps_rows))
```

```python
import functools
import math

import jax
import jax.numpy as jnp
from jax import lax
from jax.experimental import pallas as pl
from jax.experimental.pallas import tpu as pltpu

D_MODEL = 1024
N_HEADS = 4
HEAD_DIM = 64
V_DIM = 2 * HEAD_DIM
ATT_WIDTH = N_HEADS * V_DIM
QK_WIDTH = N_HEADS * 2 * HEAD_DIM
ROT_DIM = HEAD_DIM // 4
ROT_HALF = ROT_DIM // 2
ROPE_THETA = 500000.0
ATT_SCALE = HEAD_DIM ** -0.5
POOL_WINDOWS = (2, 4, 8, 16)
N_POOL_GROUPS = len(POOL_WINDOWS)
POOL_WIDTH = D_MODEL - ATT_WIDTH
POOL_GROUP = POOL_WIDTH // N_POOL_GROUPS
POOL_HIST = max(POOL_WINDOWS) - 1
HIST_PAD = POOL_HIST + 1
D_FF = 2816
FF_CHUNK = 1408
PLE_DIM = 256
EPS = 1e-6
NEG = -1e30
PAGE_SIZE = 128

LANES = 128
ROW_TILE = 512
ATT_TILE = 512
PAGES_PER_STEP = 16
NEW_PAD = 16
VMEM_LIMIT = 56 * 1024 * 1024

F32 = jnp.float32
BF16 = jnp.bfloat16


def _dot(a, b):
    return jnp.dot(a, b, preferred_element_type=F32)


def _dot_nt(a, b):
    return lax.dot_general(a, b, (((1,), (1,)), ((), ())), preferred_element_type=F32)


def _rms(v, g):
    return v * lax.rsqrt(jnp.mean(v * v, axis=-1, keepdims=True) + EPS) * g


def _lam(lp_ref, lam_init):
    lp = lp_ref[...]
    s1 = jnp.sum(lp[0:1, :] * lp[1:2, :], axis=1, keepdims=True)
    s2 = jnp.sum(lp[2:3, :] * lp[3:4, :], axis=1, keepdims=True)
    return jnp.exp(s1) - jnp.exp(s2) + lam_init


def _subln(o, gsub, lam_init):
    y = o * lax.rsqrt(jnp.mean(o * o, axis=-1, keepdims=True) + EPS)
    return y * gsub * (1.0 - lam_init)


def _proj_kernel(x_ref, g_ref, w_ref, ta_ref, tb_ref, tc_ref,
                 q_ref, kf_ref, kb_ref, vf_ref, vb_ref, u_ref):
    h = _rms(x_ref[...], g_ref[...]).astype(BF16)
    ta, tb, tc = ta_ref[...], tb_ref[...], tc_ref[...]

    def rope(z):
        return z * ta + pltpu.roll(z, LANES - ROT_HALF, 1) * tb + pltpu.roll(z, ROT_HALF, 1) * tc

    zq = _dot(h, w_ref[:, 0:QK_WIDTH])
    for hh in range(N_HEADS):
        sl = slice(hh * V_DIM, (hh + 1) * V_DIM)
        q_ref[:, sl] = (rope(zq[:, sl]) * ATT_SCALE).astype(BF16)
    zk = _dot(h, w_ref[:, QK_WIDTH:2 * QK_WIDTH])
    for hh in range(N_HEADS):
        sl = slice(hh * V_DIM, (hh + 1) * V_DIM)
        k = rope(zk[:, sl])
        kf_ref[:, sl] = k
        kb_ref[:, sl] = k.astype(BF16)
    zv = _dot(h, w_ref[:, 2 * QK_WIDTH:2 * QK_WIDTH + ATT_WIDTH])
    vf_ref[...] = zv
    vb_ref[...] = zv.astype(BF16)
    u_ref[...] = _dot(h, w_ref[:, 2 * QK_WIDTH + ATT_WIDTH:])


def _proj(x, g, w_in, tabs, tm, tiles_per_seq):
    m = x.shape[0]
    row = lambda i: (i, 0)
    const = lambda i: (0, 0)
    tab = lambda i: (i % tiles_per_seq, 0)
    out_f = jax.ShapeDtypeStruct((m, ATT_WIDTH), F32)
    out_b = jax.ShapeDtypeStruct((m, ATT_WIDTH), BF16)
    wide = pl.BlockSpec((tm, ATT_WIDTH), row)
    return pl.pallas_call(
        _proj_kernel,
        grid=(m // tm,),
        in_specs=[
            pl.BlockSpec((tm, D_MODEL), row),
            pl.BlockSpec((1, D_MODEL), const),
            pl.BlockSpec(w_in.shape, const),
            pl.BlockSpec((tm, LANES), tab),
            pl.BlockSpec((tm, LANES), tab),
            pl.BlockSpec((tm, LANES), tab),
        ],
        out_specs=[wide] * 6,
        out_shape=[out_b, out_f, out_b, out_f, out_b, out_f],
        compiler_params=pltpu.CompilerParams(
            dimension_semantics=("arbitrary",), vmem_limit_bytes=VMEM_LIMIT),
        name="proj",
    )(x, g, w_in, *tabs)


def _rope_tables(pos):
    t = pos.shape[0]
    inv = jnp.power(ROPE_THETA, -jnp.arange(0, ROT_DIM, 2, dtype=F32) / ROT_DIM)
    ang = pos.astype(F32)[:, None] * inv[None, :]
    cos, sin = jnp.cos(ang), jnp.sin(ang)
    rest = HEAD_DIM - ROT_DIM
    ta = jnp.concatenate([cos, cos, jnp.ones((t, rest), F32)], axis=-1)
    tb = jnp.concatenate([-sin, jnp.zeros((t, rest + ROT_HALF), F32)], axis=-1)
    tc = jnp.concatenate([jnp.zeros((t, ROT_HALF), F32), sin, jnp.zeros((t, rest), F32)], axis=-1)
    return tuple(jnp.concatenate([a, a], axis=-1) for a in (ta, tb, tc))


def _attn_prompt_kernel(lp_ref, gsub_ref, q_ref, k_ref, v_ref, o_ref,
                        qs_ref, m_ref, l_ref, acc_ref, *, lam_init, tile):
    qi = pl.program_id(2)
    lane = lax.broadcasted_iota(jnp.int32, (tile, V_DIM), 1)
    q = q_ref[...]
    zero = jnp.zeros_like(q)
    qs_ref[0:tile, :] = jnp.where(lane < HEAD_DIM, q, zero)
    qs_ref[tile:, :] = jnp.where(lane >= HEAD_DIM, q, zero)
    m_ref[...] = jnp.full(m_ref.shape, NEG, F32)
    l_ref[...] = jnp.zeros(l_ref.shape, F32)
    acc_ref[...] = jnp.zeros(acc_ref.shape, F32)

    def step(ki, masked):
        start = pl.multiple_of(ki * tile, tile)
        k = k_ref[pl.ds(start, tile), :]
        v = v_ref[pl.ds(start, tile), :]
        s = _dot_nt(qs_ref[...], k)
        if masked:
            row = lax.broadcasted_iota(jnp.int32, s.shape, 0)
            col = lax.broadcasted_iota(jnp.int32, s.shape, 1)
            qpos = jnp.where(row >= tile, row - tile, row)
            s = jnp.where(col <= qpos, s, NEG)
        m_old = m_ref[...]
        m_new = jnp.maximum(m_old, jnp.max(s, axis=1, keepdims=True))
        alpha = jnp.exp(m_old - m_new)
        p = jnp.exp(s - m_new)
        l_ref[...] = alpha * l_ref[...] + jnp.sum(p, axis=1, keepdims=True)
        acc_ref[...] = alpha * acc_ref[...] + _dot(p.astype(BF16), v)
        m_ref[...] = m_new

    def body(ki, carry):
        step(ki, False)
        return carry

    lax.fori_loop(0, qi, body, 0)
    step(qi, True)

    lam = _lam(lp_ref, lam_init)
    acc = acc_ref[...]
    l = l_ref[...]
    o = acc[0:tile] / l[0:tile] - lam * (acc[tile:] / l[tile:])
    o_ref[...] = _subln(o, gsub_ref[...], lam_init).astype(o_ref.dtype)


def _attn_prompt(q, k, v, lp, gsub, lam_init, batch, seq):
    tile = ATT_TILE
    nq = seq // tile
    kern = functools.partial(_attn_prompt_kernel, lam_init=lam_init, tile=tile)
    kv_spec = pl.BlockSpec((seq, V_DIM), lambda b, h, i: (b, h))
    return pl.pallas_call(
        kern,
        grid=(batch, N_HEADS, nq),
        in_specs=[
            pl.BlockSpec(lp.shape, lambda b, h, i: (0, 0)),
            pl.BlockSpec((1, V_DIM), lambda b, h, i: (0, 0)),
            pl.BlockSpec((tile, V_DIM), lambda b, h, i: (b * nq + i, h)),
            kv_spec,
            kv_spec,
        ],
        out_specs=pl.BlockSpec((tile, V_DIM), lambda b, h, i: (b * nq + i, h)),
        out_shape=jax.ShapeDtypeStruct((batch * seq, ATT_WIDTH), BF16),
        scratch_shapes=[
            pltpu.VMEM((2 * tile, V_DIM), BF16),
            pltpu.VMEM((2 * tile, 1), F32),
            pltpu.VMEM((2 * tile, 1), F32),
            pltpu.VMEM((2 * tile, V_DIM), F32),
        ],
        compiler_params=pltpu.CompilerParams(
            dimension_semantics=("arbitrary", "arbitrary", "arbitrary"),
            vmem_limit_bytes=VMEM_LIMIT),
        name="attn_prompt",
    )(lp, gsub, q, k, v)


def _pool_prompt_kernel(u_ref, prev_ref, w_ref, sc_ref, o_ref, ext_ref, *, tm):
    i = pl.program_id(1)
    prev = prev_ref[...]
    ext_ref[0:HIST_PAD, :] = jnp.where(i == 0, jnp.zeros_like(prev), prev)
    ext_ref[HIST_PAD:, :] = u_ref[...]
    pos = i * tm + lax.broadcasted_iota(jnp.int32, (tm, POOL_GROUP), 0)
    for g, w in enumerate(POOL_WINDOWS):
        sl = slice(g * POOL_GROUP, (g + 1) * POOL_GROUP)
        cur = ext_ref[HIST_PAD:, sl]
        tot = cur
        for j in range(1, w):
            tot = tot + ext_ref[HIST_PAD - j:HIST_PAD - j + tm, sl]
        cnt = jnp.minimum(pos + 1, w).astype(F32)
        pooled = (tot / cnt - cur).astype(BF16)
        o_ref[:, sl] = (_dot(pooled, w_ref[g]) * sc_ref[:, sl]).astype(o_ref.dtype)


def _pool_prompt(u, w_pool, scale, batch, seq):
    tm = ROW_TILE
    nt = seq // tm
    per = tm // HIST_PAD
    kern = functools.partial(_pool_prompt_kernel, tm=tm)
    return pl.pallas_call(
        kern,
        grid=(batch, nt),
        in_specs=[
            pl.BlockSpec((tm, POOL_WIDTH), lambda b, i: (b * nt + i, 0)),
            pl.BlockSpec((HIST_PAD, POOL_WIDTH),
                         lambda b, i: (jnp.maximum((b * nt + i) * per - 1, 0), 0)),
            pl.BlockSpec(w_pool.shape, lambda b, i: (0, 0, 0)),
            pl.BlockSpec((1, POOL_WIDTH), lambda b, i: (0, 0)),
        ],
        out_specs=pl.BlockSpec((tm, POOL_WIDTH), lambda b, i: (b * nt + i, 0)),
        out_shape=jax.ShapeDtypeStruct((batch * seq, POOL_WIDTH), BF16),
        scratch_shapes=[pltpu.VMEM((tm + HIST_PAD, POOL_WIDTH), F32)],
        compiler_params=pltpu.CompilerParams(
            dimension_semantics=("arbitrary", "arbitrary"), vmem_limit_bytes=VMEM_LIMIT),
        name="pool_prompt",
    )(u, u, w_pool, scale)


def _pool_sample_kernel(hist_ref, u_ref, w_ref, sc_ref, o_ref, *, t_new, pos0):
    ext = [hist_ref[j] for j in range(POOL_HIST)] + [u_ref[t] for t in range(t_new)]
    for g, w in enumerate(POOL_WINDOWS):
        sl = slice(g * POOL_GROUP, (g + 1) * POOL_GROUP)
        rows = []
        for t in range(t_new):
            cur = ext[POOL_HIST + t][:, sl]
            tot = cur
            for j in range(1, w):
                tot = tot + ext[POOL_HIST + t - j][:, sl]
            cnt = float(min(pos0 + t + 1, w))
            rows.append(tot / cnt - cur)
        pooled = jnp.concatenate(rows, axis=0).astype(BF16)
        y = _dot(pooled, w_ref[g]) * sc_ref[:, sl]
        nb = y.shape[0] // t_new
        for t in range(t_new):
            o_ref[t, :, sl] = y[t * nb:(t + 1) * nb]


def _pool_sample(hist_tm, u_tm, w_pool, scale, pos0):
    t_new, nb, _ = u_tm.shape
    kern = functools.partial(_pool_sample_kernel, t_new=t_new, pos0=pos0)
    return pl.pallas_call(
        kern,
        out_shape=jax.ShapeDtypeStruct((t_new, nb, POOL_WIDTH), F32),
        name="pool_sample",
    )(hist_tm, u_tm, w_pool, scale)


def _attn_sample_kernel(pt_ref, lp_ref, gsub_ref, w_ref, kn_ref, vn_ref, *rest,
                        lam_init, t_new, n_groups):
    g_pages = PAGES_PER_STEP
    k_refs = rest[:g_pages]
    v_refs = rest[g_pages:2 * g_pages]
    o_ref, m_ref, l_ref, acc_ref, ks_ref, vs_ref = rest[2 * g_pages:]
    g = pl.program_id(1)
    w = w_ref[0]
    n_rows = w.shape[0]

    @pl.when(g == 0)
    def _init():
        sn = _dot_nt(w, kn_ref[0])
        row = lax.broadcasted_iota(jnp.int32, sn.shape, 0)
        col = lax.broadcasted_iota(jnp.int32, sn.shape, 1)
        valid = col <= (row & (t_new - 1))
        sn = jnp.where(valid, sn, NEG)
        m = jnp.max(sn, axis=1, keepdims=True)
        p = jnp.where(valid, jnp.exp(sn - m), 0.0)
        m_ref[...] = m
        l_ref[...] = jnp.sum(p, axis=1, keepdims=True)
        acc_ref[...] = _dot(p.astype(BF16), vn_ref[0])

    for j in range(g_pages):
        ks_ref[j * PAGE_SIZE:(j + 1) * PAGE_SIZE, :] = k_refs[j][...].astype(BF16)
        vs_ref[j * PAGE_SIZE:(j + 1) * PAGE_SIZE, :] = v_refs[j][...].astype(BF16)
    s = _dot_nt(w, ks_ref[...])
    m_old = m_ref[...]
    m_new = jnp.maximum(m_old, jnp.max(s, axis=1, keepdims=True))
    alpha = jnp.exp(m_old - m_new)
    p = jnp.exp(s - m_new)
    l_ref[...] = alpha * l_ref[...] + jnp.sum(p, axis=1, keepdims=True)
    acc_ref[...] = alpha * acc_ref[...] + _dot(p.astype(BF16), vs_ref[...])
    m_ref[...] = m_new

    @pl.when(g == n_groups - 1)
    def _fin():
        lam = _lam(lp_ref, lam_init)
        accn = acc_ref[...] / l_ref[...]
        per = n_rows // N_HEADS
        for hh in range(N_HEADS):
            blk = accn[hh * per:(hh + 1) * per, hh * V_DIM:(hh + 1) * V_DIM]
            o = blk[0:t_new] - lam * blk[t_new:2 * t_new]
            o_ref[0, :, hh * V_DIM:(hh + 1) * V_DIM] = _subln(o, gsub_ref[...], lam_init)


def _attn_sample(page_table, lp, gsub, w_blk, k_new, v_new, cache_k, cache_v, layer, lam_init,
                 t_new):
    nb, n_pages = page_table.shape
    g_pages = PAGES_PER_STEP
    n_groups = n_pages // g_pages
    n_rows = w_blk.shape[1]
    width = cache_k.shape[-1]
    kern = functools.partial(_attn_sample_kernel, lam_init=lam_init, t_new=t_new,
                             n_groups=n_groups)

    def page_spec(j):
        return pl.BlockSpec((None, None, PAGE_SIZE, width),
                            lambda b, g, pt: (layer, pt[b, g * g_pages + j], 0, 0))

    per_seq = lambda b, g, pt: (b, 0, 0)
    grid_spec = pltpu.PrefetchScalarGridSpec(
        num_scalar_prefetch=1,
        grid=(nb, n_groups),
        in_specs=[
            pl.BlockSpec(lp.shape, lambda b, g, pt: (0, 0)),
            pl.BlockSpec((1, V_DIM), lambda b, g, pt: (0, 0)),
            pl.BlockSpec((1, n_rows, width), per_seq),
            pl.BlockSpec((1, NEW_PAD, width), per_seq),
            pl.BlockSpec((1, NEW_PAD, width), per_seq),
        ] + [page_spec(j) for j in range(g_pages)] * 2,
        out_specs=pl.BlockSpec((1, t_new, width), per_seq),
        scratch_shapes=[
            pltpu.VMEM((n_rows, 1), F32),
            pltpu.VMEM((n_rows, 1), F32),
            pltpu.VMEM((n_rows, width), F32),
            pltpu.VMEM((g_pages * PAGE_SIZE, width), BF16),
            pltpu.VMEM((g_pages * PAGE_SIZE, width), BF16),
        ],
    )
    return pl.pallas_call(
        kern,
        grid_spec=grid_spec,
        out_shape=jax.ShapeDtypeStruct((nb, t_new, width), F32),
        compiler_params=pltpu.CompilerParams(
            dimension_semantics=("arbitrary", "arbitrary"), vmem_limit_bytes=VMEM_LIMIT),
        name="attn_sample",
    )(page_table, lp, gsub, w_blk, k_new, v_new,
      *([cache_k] * g_pages), *([cache_v] * g_pages))


def _finish_kernel(x_ref, att_ref, pool_ref, p_ref, wo_ref, gpm_ref, gpf_ref, gqf_ref,
                   wg_ref, wu_ref, wd_ref, wpg_ref, wpp_ref, o_ref):
    mix = (_dot(att_ref[...].astype(BF16), wo_ref[0:ATT_WIDTH, :])
           + _dot(pool_ref[...].astype(BF16), wo_ref[ATT_WIDTH:, :]))
    x1 = x_ref[...] + _rms(mix, gpm_ref[...])
    h = _rms(x1, gpf_ref[...]).astype(BF16)
    f = None
    for c in range(D_FF // FF_CHUNK):
        sl = slice(c * FF_CHUNK, (c + 1) * FF_CHUNK)
        a = (jax.nn.silu(_dot(h, wg_ref[:, sl])) * _dot(h, wu_ref[:, sl])).astype(BF16)
        part = _dot(a, wd_ref[sl, :])
        f = part if f is None else f + part
    x2 = x1 + _rms(f, gqf_ref[...])
    gate = jax.nn.sigmoid(_dot(x2.astype(BF16), wpg_ref[...]))
    o_ref[...] = x2 + gate * _dot(p_ref[...].astype(BF16), wpp_ref[...])


def _finish(x, att, pool, p, w_o, g_post_mix, g_pre_ffn, g_post_ffn, w_gate, w_up, w_down,
            w_ple_gate, w_ple_proj, tm):
    m = x.shape[0]
    row = lambda i: (i, 0)
    const = lambda i: (0, 0)
    resident = lambda a: pl.BlockSpec(a.shape, const, pipeline_mode=pl.Buffered(1))
    return pl.pallas_call(
        _finish_kernel,
        grid=(m // tm,),
        in_specs=[
            pl.BlockSpec((tm, D_MODEL), row),
            pl.BlockSpec((tm, ATT_WIDTH), row),
            pl.BlockSpec((tm, POOL_WIDTH), row),
            pl.BlockSpec((tm, PLE_DIM), row),
            resident(w_o), resident(g_post_mix), resident(g_pre_ffn), resident(g_post_ffn),
            resident(w_gate), resident(w_up), resident(w_down),
            resident(w_ple_gate), resident(w_ple_proj),
        ],
        out_specs=pl.BlockSpec((tm, D_MODEL), row),
        out_shape=jax.ShapeDtypeStruct((m, D_MODEL), F32),
        compiler_params=pltpu.CompilerParams(
            dimension_semantics=("arbitrary",), vmem_limit_bytes=VMEM_LIMIT),
        name="finish",
    )(x, att, pool, p, w_o, g_post_mix, g_pre_ffn, g_post_ffn, w_gate, w_up, w_down,
      w_ple_gate, w_ple_proj)


def _block_diag_queries(q, nb, t_new):
    qr = q.reshape(nb, t_new, N_HEADS * 2, HEAD_DIM).transpose(0, 2, 1, 3)
    eye = jnp.eye(N_HEADS * 2, dtype=q.dtype)
    w = qr[:, :, :, None, :] * eye[None, :, None, :, None]
    return w.reshape(nb, N_HEADS * 2 * t_new, QK_WIDTH)


def kernel(x_prompt, x_sample, p_prompt, p_sample, cache_k, cache_v, state_pool, page_table, w_in, lambda_params, g_sub, w_pool, pool_scale, w_o, g_pre_mix, g_post_mix, g_pre_ffn, g_post_ffn, w_gate, w_up, w_down, w_ple_gate, w_ple_proj):
    batch, seq, _ = x_prompt.shape
    nb, t_new, _ = x_sample.shape
    depth = w_in.shape[0]
    n_pages = page_table.shape[1]
    past_len = n_pages * PAGE_SIZE
    assert seq % ROW_TILE == 0 and seq % ATT_TILE == 0 and n_pages % PAGES_PER_STEP == 0
    assert t_new & (t_new - 1) == 0 and t_new <= NEW_PAD

    tabs_p = _rope_tables(jnp.arange(seq))
    pos_s = past_len + jnp.arange(t_new)
    tabs_s = tuple(jnp.tile(a, (nb, 1)) for a in _rope_tables(pos_s))
    ck = cache_k.reshape(cache_k.shape[0], cache_k.shape[1], PAGE_SIZE, QK_WIDTH)
    cv = cache_v.reshape(cache_v.shape[0], cache_v.shape[1], PAGE_SIZE, ATT_WIDTH)
    bf = lambda a: a.astype(BF16)
    row2 = lambda a: a.reshape(1, -1)

    xp = x_prompt.reshape(batch * seq, D_MODEL)
    xs = x_sample.reshape(nb * t_new, D_MODEL)
    kp_rows, vp_rows, pp_rows, ks_rows, vs_rows, ps_rows = [], [], [], [], [], []
    for i in range(depth):
        lam_init = 0.8 - 0.6 * math.exp(-0.3 * i)
        w_in_i, w_pool_i = bf(w_in[i]), bf(w_pool[i])
        ffn_w = (bf(w_o[i]), row2(g_post_mix[i]), row2(g_pre_ffn[i]), row2(g_post_ffn[i]),
                 bf(w_gate[i]), bf(w_up[i]), bf(w_down[i]), bf(w_ple_gate[i]), bf(w_ple_proj[i]))
        g_in, gsub_i, scale_i = row2(g_pre_mix[i]), row2(g_sub[i]), row2(pool_scale[i])
        lp = lambda_params[i]

        q, kf, kb, vf, vb, u = _proj(xp, g_in, w_in_i, tabs_p, ROW_TILE, seq // ROW_TILE)
        att = _attn_prompt(q, kb, vb, lp, gsub_i, lam_init, batch, seq)
        pool = _pool_prompt(u, w_pool_i, scale_i, batch, seq)
        xp = _finish(xp, att, pool, p_prompt[i].reshape(batch * seq, PLE_DIM), *ffn_w, ROW_TILE)
        kp_rows.append(kf.reshape(batch, seq, N_HEADS, V_DIM))
        vp_rows.append(vf.reshape(batch, seq, N_HEADS, V_DIM))
        pp_rows.append(u.reshape(batch, seq, POOL_WIDTH)[:, seq - POOL_HIST:])

        q, kf, kb, vf, vb, u = _proj(xs, g_in, w_in_i, tabs_s, nb * t_new, 1)
        pad = lambda a: jnp.pad(a.reshape(nb, t_new, -1), ((0, 0), (0, NEW_PAD - t_new), (0, 0)))
        att = _attn_sample(page_table, lp, gsub_i, _block_diag_queries(q, nb, t_new), pad(kb),
                           pad(vb), ck, cv, i, lam_init, t_new)
        u3 = u.reshape(nb, t_new, POOL_WIDTH)
        pool = _pool_sample(state_pool[i].transpose(1, 0, 2), u3.transpose(1, 0, 2), w_pool_i,
                            scale_i, past_len)
        pool = pool.transpose(1, 0, 2).reshape(nb * t_new, POOL_WIDTH)
        xs = _finish(xs, att.reshape(nb * t_new, ATT_WIDTH), pool,
                     p_sample[i].reshape(nb * t_new, PLE_DIM), *ffn_w, nb * t_new)
        ks_rows.append(kf.reshape(nb, t_new, N_HEADS, V_DIM))
        vs_rows.append(vf.reshape(nb, t_new, N_HEADS, V_DIM))
        ps_rows.append(jnp.concatenate([state_pool[i], u3], axis=1)[:, t_new:])

    return (xp.reshape(batch, seq, D_MODEL), xs.reshape(nb, t_new, D_MODEL),
            jnp.stack(kp_rows), jnp.stack(vp_rows), jnp.stack(pp_rows),
            jnp.stack(ks_rows), jnp.stack(vs_rows), jnp.stack(ps_rows))
```

```python
import functools
import math

import jax
import jax.numpy as jnp
from jax import lax
from jax.experimental import pallas as pl
from jax.experimental.pallas import tpu as pltpu

D_MODEL = 1024
N_HEADS = 4
HEAD_DIM = 64
V_DIM = 2 * HEAD_DIM
ATT_WIDTH = N_HEADS * V_DIM
QK_WIDTH = N_HEADS * 2 * HEAD_DIM
ROT_DIM = HEAD_DIM // 4
ROT_HALF = ROT_DIM // 2
ROPE_THETA = 500000.0
ATT_SCALE = HEAD_DIM ** -0.5
Q_SCALE = ATT_SCALE * math.log2(math.e)
POOL_WINDOWS = (2, 4, 8, 16)
N_POOL_GROUPS = len(POOL_WINDOWS)
POOL_WIDTH = D_MODEL - ATT_WIDTH
POOL_GROUP = POOL_WIDTH // N_POOL_GROUPS
POOL_HIST = max(POOL_WINDOWS) - 1
HIST_PAD = POOL_HIST + 1
D_FF = 2816
FF_CHUNK = 1408
PLE_DIM = 256
EPS = 1e-6
NEG = -1e30
PAGE_SIZE = 128
PAGE_ROWS = PAGE_SIZE * N_HEADS

LANES = 128
ROW_TILE = 512
Q_TILE = 512
KV_TILE = 512
ONES_ROWS = 16
VT_ROWS = V_DIM + ONES_ROWS
HEADS_PER_STEP = 2
PAGES_PER_STEP = 16
VMEM_LIMIT = 56 * 1024 * 1024

F32 = jnp.float32
BF16 = jnp.bfloat16


def _dot(a, b):
    return jnp.dot(a, b, preferred_element_type=F32)


def _dot_nt(a, b):
    return lax.dot_general(a, b, (((1,), (1,)), ((), ())), preferred_element_type=F32)


def _rms(v, g):
    return v * lax.rsqrt(jnp.mean(v * v, axis=-1, keepdims=True) + EPS) * g


def _lam(lp_ref, lam_init):
    lp = lp_ref[...]
    s1 = jnp.sum(lp[0:1, :] * lp[1:2, :], axis=1, keepdims=True)
    s2 = jnp.sum(lp[2:3, :] * lp[3:4, :], axis=1, keepdims=True)
    return jnp.exp(s1) - jnp.exp(s2) + lam_init


def _subln(o, gsub, lam_init):
    y = o * lax.rsqrt(jnp.mean(o * o, axis=-1, keepdims=True) + EPS)
    return y * gsub * (1.0 - lam_init)


def _proj_kernel(x_ref, g_ref, w_ref, ta_ref, tb_ref, tc_ref,
                 q_ref, kf_ref, vf_ref, u_ref, *attn_refs, tm):
    h = _rms(x_ref[...], g_ref[...]).astype(BF16)
    ta, tb, tc = ta_ref[...], tb_ref[...], tc_ref[...]

    def rope(z):
        return z * ta + pltpu.roll(z, LANES - ROT_HALF, 1) * tb + pltpu.roll(z, ROT_HALF, 1) * tc

    def head_rows(hh):
        return pl.ds(hh, tm, stride=N_HEADS)

    zq = _dot(h, w_ref[:, 0:QK_WIDTH])
    for hh in range(N_HEADS):
        sl = slice(hh * V_DIM, (hh + 1) * V_DIM)
        q_ref[:, sl] = (rope(zq[:, sl]) * Q_SCALE).astype(BF16)
    zk = _dot(h, w_ref[:, QK_WIDTH:2 * QK_WIDTH])
    for hh in range(N_HEADS):
        sl = slice(hh * V_DIM, (hh + 1) * V_DIM)
        k = rope(zk[:, sl])
        kf_ref[head_rows(hh), :] = k
        if attn_refs:
            attn_refs[0][:, sl] = k.astype(BF16)
    zv = _dot(h, w_ref[:, 2 * QK_WIDTH:2 * QK_WIDTH + ATT_WIDTH])
    for hh in range(N_HEADS):
        vf_ref[head_rows(hh), :] = zv[:, hh * V_DIM:(hh + 1) * V_DIM]
    if attn_refs:
        vt_ref = attn_refs[1]
        ones = jnp.ones((ONES_ROWS, KV_TILE), BF16)
        for c in range(tm // KV_TILE):
            vt = zv[c * KV_TILE:(c + 1) * KV_TILE, :].T.astype(BF16)
            for hh in range(N_HEADS):
                vt_ref[c, hh * VT_ROWS:hh * VT_ROWS + V_DIM, :] = vt[hh * V_DIM:(hh + 1) * V_DIM]
                vt_ref[c, hh * VT_ROWS + V_DIM:(hh + 1) * VT_ROWS, :] = ones
    u_ref[...] = _dot(h, w_ref[:, 2 * QK_WIDTH + ATT_WIDTH:])


def _proj(x, g, w_in, tabs, tm, tiles_per_seq, for_prompt):
    m = x.shape[0]
    row = lambda i: (i, 0)
    const = lambda i: (0, 0)
    tab = lambda i: (i % tiles_per_seq, 0)
    wide = pl.BlockSpec((tm, ATT_WIDTH), row)
    head_major = pl.BlockSpec((tm * N_HEADS, V_DIM), row)
    rows_f32 = jax.ShapeDtypeStruct((m * N_HEADS, V_DIM), F32)
    out_specs = [wide, head_major, head_major, wide]
    out_shape = [jax.ShapeDtypeStruct((m, QK_WIDTH), BF16), rows_f32, rows_f32,
                 jax.ShapeDtypeStruct((m, POOL_WIDTH), F32)]
    if for_prompt:
        per = tm // KV_TILE
        out_specs += [wide, pl.BlockSpec((per, N_HEADS * VT_ROWS, KV_TILE), lambda i: (i, 0, 0))]
        out_shape += [jax.ShapeDtypeStruct((m, QK_WIDTH), BF16),
                      jax.ShapeDtypeStruct((m // KV_TILE, N_HEADS * VT_ROWS, KV_TILE), BF16)]
    return pl.pallas_call(
        functools.partial(_proj_kernel, tm=tm),
        grid=(m // tm,),
        in_specs=[
            pl.BlockSpec((tm, D_MODEL), row),
            pl.BlockSpec((1, D_MODEL), const),
            pl.BlockSpec(w_in.shape, const),
            pl.BlockSpec((tm, LANES), tab),
            pl.BlockSpec((tm, LANES), tab),
            pl.BlockSpec((tm, LANES), tab),
        ],
        out_specs=out_specs,
        out_shape=out_shape,
        compiler_params=pltpu.CompilerParams(
            dimension_semantics=("arbitrary",), vmem_limit_bytes=VMEM_LIMIT),
        name="proj",
    )(x, g, w_in, *tabs)


def _rope_tables(pos):
    t = pos.shape[0]
    inv = jnp.power(ROPE_THETA, -jnp.arange(0, ROT_DIM, 2, dtype=F32) / ROT_DIM)
    ang = pos.astype(F32)[:, None] * inv[None, :]
    cos, sin = jnp.cos(ang), jnp.sin(ang)
    rest = HEAD_DIM - ROT_DIM
    ta = jnp.concatenate([cos, cos, jnp.ones((t, rest), F32)], axis=-1)
    tb = jnp.concatenate([-sin, jnp.zeros((t, rest + ROT_HALF), F32)], axis=-1)
    tc = jnp.concatenate([jnp.zeros((t, ROT_HALF), F32), sin, jnp.zeros((t, rest), F32)], axis=-1)
    return tuple(jnp.concatenate([a, a], axis=-1) for a in (ta, tb, tc))


def _attn_prompt_kernel(lp_ref, gsub_ref, q_ref, k_ref, vt_ref, o_ref,
                        qst_ref, m_ref, acc_ref, *, lam_init, tq, tk, nh):
    qi = pl.program_id(2)
    comp = lax.broadcasted_iota(jnp.int32, (V_DIM, tq), 0) < HEAD_DIM
    for hh in range(nh):
        qt = q_ref[:, hh * V_DIM:(hh + 1) * V_DIM].astype(F32).T
        qst_ref[hh, :, 0:tq] = jnp.where(comp, qt, 0.0).astype(BF16)
        qst_ref[hh, :, tq:] = jnp.where(comp, 0.0, qt).astype(BF16)
    m_ref[...] = jnp.full(m_ref.shape, NEG, F32)
    acc_ref[...] = jnp.zeros(acc_ref.shape, F32)

    def step(j, mask):
        start = pl.multiple_of(j * tk, tk)
        scores = []
        for hh in range(nh):
            s = _dot(k_ref[pl.ds(start, tk), hh * V_DIM:(hh + 1) * V_DIM], qst_ref[hh])
            scores.append(s if mask is None else jnp.where(mask, s, NEG))
        probs = []
        for hh in range(nh):
            m_old = m_ref[hh]
            m_new = jnp.maximum(m_old, jnp.max(scores[hh], axis=0, keepdims=True))
            m_ref[hh] = m_new
            probs.append((jnp.exp2(m_old - m_new), jnp.exp2(scores[hh] - m_new).astype(BF16)))
        for hh in range(nh):
            alpha, p = probs[hh]
            acc_ref[hh] = alpha * acc_ref[hh] + _dot(vt_ref[j, hh * VT_ROWS:(hh + 1) * VT_ROWS, :], p)

    per = tq // tk

    def body(j, carry):
        step(j, None)
        return carry

    lax.fori_loop(0, qi * per, body, 0)
    row = lax.broadcasted_iota(jnp.int32, (tk, 2 * tq), 0)
    col = lax.broadcasted_iota(jnp.int32, (tk, 2 * tq), 1)
    for d in range(per):
        step(qi * per + d, (d * tk + row) <= (col & (tq - 1)))

    lam = _lam(lp_ref, lam_init)
    for hh in range(nh):
        acc = acc_ref[hh, 0:V_DIM, :]
        l = acc_ref[hh, V_DIM:V_DIM + 1, :]
        ot = acc[:, 0:tq] / l[:, 0:tq] - lam * (acc[:, tq:] / l[:, tq:])
        o_ref[:, hh * V_DIM:(hh + 1) * V_DIM] = _subln(ot.T, gsub_ref[...], lam_init).astype(o_ref.dtype)


def _attn_prompt(q, k, vt, lp, gsub, lam_init, batch, seq):
    tq, tk, nh = Q_TILE, KV_TILE, HEADS_PER_STEP
    nq = seq // tq
    kern = functools.partial(_attn_prompt_kernel, lam_init=lam_init, tq=tq, tk=tk, nh=nh)
    q_spec = pl.BlockSpec((tq, nh * V_DIM), lambda b, h, i: (b * nq + i, h))
    return pl.pallas_call(
        kern,
        grid=(batch, N_HEADS // nh, nq),
        in_specs=[
            pl.BlockSpec(lp.shape, lambda b, h, i: (0, 0)),
            pl.BlockSpec((1, V_DIM), lambda b, h, i: (0, 0)),
            q_spec,
            pl.BlockSpec((seq, nh * V_DIM), lambda b, h, i: (b, h)),
            pl.BlockSpec((seq // tk, nh * VT_ROWS, tk), lambda b, h, i: (b, h, 0)),
        ],
        out_specs=q_spec,
        out_shape=jax.ShapeDtypeStruct((batch * seq, ATT_WIDTH), BF16),
        scratch_shapes=[
            pltpu.VMEM((nh, V_DIM, 2 * tq), BF16),
            pltpu.VMEM((nh, 1, 2 * tq), F32),
            pltpu.VMEM((nh, VT_ROWS, 2 * tq), F32),
        ],
        compiler_params=pltpu.CompilerParams(
            dimension_semantics=("arbitrary", "arbitrary", "arbitrary"),
            vmem_limit_bytes=VMEM_LIMIT),
        name="attn_prompt",
    )(lp, gsub, q, k, vt)


def _pool_prompt_kernel(u_ref, prev_ref, w_ref, sc_ref, o_ref, ext_ref, *, tm):
    i = pl.program_id(1)
    prev = prev_ref[...]
    ext_ref[0:HIST_PAD, :] = jnp.where(i == 0, jnp.zeros_like(prev), prev)
    ext_ref[HIST_PAD:, :] = u_ref[...]
    pos = i * tm + lax.broadcasted_iota(jnp.int32, (tm, POOL_GROUP), 0)
    for g, w in enumerate(POOL_WINDOWS):
        sl = slice(g * POOL_GROUP, (g + 1) * POOL_GROUP)
        cur = ext_ref[HIST_PAD:, sl]
        tot = cur
        for j in range(1, w):
            tot = tot + ext_ref[HIST_PAD - j:HIST_PAD - j + tm, sl]
        cnt = jnp.minimum(pos + 1, w).astype(F32)
        pooled = (tot / cnt - cur).astype(BF16)
        o_ref[:, sl] = (_dot(pooled, w_ref[g]) * sc_ref[:, sl]).astype(o_ref.dtype)


def _pool_prompt(u, w_pool, scale, batch, seq):
    tm = ROW_TILE
    nt = seq // tm
    per = tm // HIST_PAD
    kern = functools.partial(_pool_prompt_kernel, tm=tm)
    return pl.pallas_call(
        kern,
        grid=(batch, nt),
        in_specs=[
            pl.BlockSpec((tm, POOL_WIDTH), lambda b, i: (b * nt + i, 0)),
            pl.BlockSpec((HIST_PAD, POOL_WIDTH),
                         lambda b, i: (jnp.maximum((b * nt + i) * per - 1, 0), 0)),
            pl.BlockSpec(w_pool.shape, lambda b, i: (0, 0, 0)),
            pl.BlockSpec((1, POOL_WIDTH), lambda b, i: (0, 0)),
        ],
        out_specs=pl.BlockSpec((tm, POOL_WIDTH), lambda b, i: (b * nt + i, 0)),
        out_shape=jax.ShapeDtypeStruct((batch * seq, POOL_WIDTH), BF16),
        scratch_shapes=[pltpu.VMEM((tm + HIST_PAD, POOL_WIDTH), F32)],
        compiler_params=pltpu.CompilerParams(
            dimension_semantics=("arbitrary", "arbitrary"), vmem_limit_bytes=VMEM_LIMIT),
        name="pool_prompt",
    )(u, u, w_pool, scale)


def _pool_sample_kernel(hist_ref, u_ref, w_ref, sc_ref, o_ref, *, t_new, pos0):
    ext = [hist_ref[j] for j in range(POOL_HIST)] + [u_ref[t] for t in range(t_new)]
    for g, w in enumerate(POOL_WINDOWS):
        sl = slice(g * POOL_GROUP, (g + 1) * POOL_GROUP)
        rows = []
        for t in range(t_new):
            cur = ext[POOL_HIST + t][:, sl]
            tot = cur
            for j in range(1, w):
                tot = tot + ext[POOL_HIST + t - j][:, sl]
            cnt = float(min(pos0 + t + 1, w))
            rows.append(tot / cnt - cur)
        pooled = jnp.concatenate(rows, axis=0).astype(BF16)
        y = _dot(pooled, w_ref[g]) * sc_ref[:, sl]
        nb = y.shape[0] // t_new
        for t in range(t_new):
            o_ref[t, :, sl] = y[t * nb:(t + 1) * nb]


def _pool_sample(hist_tm, u_tm, w_pool, scale, pos0):
    t_new, nb, _ = u_tm.shape
    kern = functools.partial(_pool_sample_kernel, t_new=t_new, pos0=pos0)
    return pl.pallas_call(
        kern,
        out_shape=jax.ShapeDtypeStruct((t_new, nb, POOL_WIDTH), F32),
        name="pool_sample",
    )(hist_tm, u_tm, w_pool, scale)


def _attn_sample_kernel(pt_ref, lp_ref, gsub_ref, w_ref, kn_ref, vn_ref, *rest,
                        lam_init, t_new, n_groups):
    g_pages = PAGES_PER_STEP
    k_refs = rest[:g_pages]
    v_refs = rest[g_pages:2 * g_pages]
    o_ref, m_ref, l_ref, acc_ref, ks_ref, vs_ref = rest[2 * g_pages:]
    g = pl.program_id(1)
    w = w_ref[0]
    head_shift = (2 * t_new).bit_length() - 1
    tok_shift = N_HEADS.bit_length() - 1

    def own_head(shape):
        row = lax.broadcasted_iota(jnp.int32, shape, 0)
        col = lax.broadcasted_iota(jnp.int32, shape, 1)
        return (col & (N_HEADS - 1)) == (row >> head_shift), row, col

    @pl.when(g == 0)
    def _init():
        sn = _dot_nt(w, kn_ref[0])
        own, row, col = own_head(sn.shape)
        valid = own & ((col >> tok_shift) <= (row & (t_new - 1)))
        sn = jnp.where(valid, sn, NEG)
        m = jnp.max(sn, axis=1, keepdims=True)
        p = jnp.where(valid, jnp.exp2(sn - m), 0.0)
        m_ref[...] = m
        l_ref[...] = jnp.sum(p, axis=1, keepdims=True)
        acc_ref[...] = _dot(p.astype(BF16), vn_ref[0])

    for j in range(g_pages):
        ks_ref[j * PAGE_ROWS:(j + 1) * PAGE_ROWS, :] = k_refs[j][...].astype(BF16)
        vs_ref[j * PAGE_ROWS:(j + 1) * PAGE_ROWS, :] = v_refs[j][...].astype(BF16)
    s = _dot_nt(w, ks_ref[...])
    s = jnp.where(own_head(s.shape)[0], s, NEG)
    m_old = m_ref[...]
    m_new = jnp.maximum(m_old, jnp.max(s, axis=1, keepdims=True))
    alpha = jnp.exp2(m_old - m_new)
    p = jnp.exp2(s - m_new)
    l_ref[...] = alpha * l_ref[...] + jnp.sum(p, axis=1, keepdims=True)
    acc_ref[...] = alpha * acc_ref[...] + _dot(p.astype(BF16), vs_ref[...])
    m_ref[...] = m_new

    @pl.when(g == n_groups - 1)
    def _fin():
        lam = _lam(lp_ref, lam_init)
        accn = acc_ref[...] / l_ref[...]
        per = 2 * t_new
        for hh in range(N_HEADS):
            blk = accn[hh * per:(hh + 1) * per]
            o = blk[0:t_new] - lam * blk[t_new:per]
            o_ref[0, :, hh * V_DIM:(hh + 1) * V_DIM] = _subln(o, gsub_ref[...], lam_init)


def _attn_sample(page_table, lp, gsub, w_blk, k_new, v_new, cache_k, cache_v, layer, lam_init,
                 t_new):
    nb, n_pages = page_table.shape
    g_pages = PAGES_PER_STEP
    n_groups = n_pages // g_pages
    n_rows = w_blk.shape[1]
    new_rows = k_new.shape[1]
    kern = functools.partial(_attn_sample_kernel, lam_init=lam_init, t_new=t_new,
                             n_groups=n_groups)

    def page_spec(j):
        return pl.BlockSpec((None, None, PAGE_ROWS, V_DIM),
                            lambda b, g, pt: (layer, pt[b, g * g_pages + j], 0, 0))

    per_seq = lambda b, g, pt: (b, 0, 0)
    grid_spec = pltpu.PrefetchScalarGridSpec(
        num_scalar_prefetch=1,
        grid=(nb, n_groups),
        in_specs=[
            pl.BlockSpec(lp.shape, lambda b, g, pt: (0, 0)),
            pl.BlockSpec((1, V_DIM), lambda b, g, pt: (0, 0)),
            pl.BlockSpec((1, n_rows, V_DIM), per_seq),
            pl.BlockSpec((1, new_rows, V_DIM), per_seq),
            pl.BlockSpec((1, new_rows, V_DIM), per_seq),
        ] + [page_spec(j) for j in range(g_pages)] * 2,
        out_specs=pl.BlockSpec((1, t_new, ATT_WIDTH), per_seq),
        scratch_shapes=[
            pltpu.VMEM((n_rows, 1), F32),
            pltpu.VMEM((n_rows, 1), F32),
            pltpu.VMEM((n_rows, V_DIM), F32),
            pltpu.VMEM((g_pages * PAGE_ROWS, V_DIM), BF16),
            pltpu.VMEM((g_pages * PAGE_ROWS, V_DIM), BF16),
        ],
    )
    return pl.pallas_call(
        kern,
        grid_spec=grid_spec,
        out_shape=jax.ShapeDtypeStruct((nb, t_new, ATT_WIDTH), F32),
        compiler_params=pltpu.CompilerParams(
            dimension_semantics=("arbitrary", "arbitrary"), vmem_limit_bytes=VMEM_LIMIT),
        name="attn_sample",
    )(page_table, lp, gsub, w_blk, k_new, v_new,
      *([cache_k] * g_pages), *([cache_v] * g_pages))


def _finish_kernel(x_ref, att_ref, pool_ref, p_ref, wo_ref, gpm_ref, gpf_ref, gqf_ref,
                   wg_ref, wu_ref, wd_ref, wpg_ref, wpp_ref, o_ref):
    mix = (_dot(att_ref[...].astype(BF16), wo_ref[0:ATT_WIDTH, :])
           + _dot(pool_ref[...].astype(BF16), wo_ref[ATT_WIDTH:, :]))
    x1 = x_ref[...] + _rms(mix, gpm_ref[...])
    h = _rms(x1, gpf_ref[...]).astype(BF16)
    f = None
    for c in range(D_FF // FF_CHUNK):
        sl = slice(c * FF_CHUNK, (c + 1) * FF_CHUNK)
        a = (jax.nn.silu(_dot(h, wg_ref[:, sl])) * _dot(h, wu_ref[:, sl])).astype(BF16)
        part = _dot(a, wd_ref[sl, :])
        f = part if f is None else f + part
    x2 = x1 + _rms(f, gqf_ref[...])
    gate = jax.nn.sigmoid(_dot(x2.astype(BF16), wpg_ref[...]))
    o_ref[...] = x2 + gate * _dot(p_ref[...].astype(BF16), wpp_ref[...])


def _finish(x, att, pool, p, w_o, g_post_mix, g_pre_ffn, g_post_ffn, w_gate, w_up, w_down,
            w_ple_gate, w_ple_proj, tm):
    m = x.shape[0]
    row = lambda i: (i, 0)
    const = lambda i: (0, 0)
    resident = lambda a: pl.BlockSpec(a.shape, const, pipeline_mode=pl.Buffered(1))
    return pl.pallas_call(
        _finish_kernel,
        grid=(m // tm,),
        in_specs=[
            pl.BlockSpec((tm, D_MODEL), row),
            pl.BlockSpec((tm, ATT_WIDTH), row),
            pl.BlockSpec((tm, POOL_WIDTH), row),
            pl.BlockSpec((tm, PLE_DIM), row),
            resident(w_o), resident(g_post_mix), resident(g_pre_ffn), resident(g_post_ffn),
            resident(w_gate), resident(w_up), resident(w_down),
            resident(w_ple_gate), resident(w_ple_proj),
        ],
        out_specs=pl.BlockSpec((tm, D_MODEL), row),
        out_shape=jax.ShapeDtypeStruct((m, D_MODEL), F32),
        compiler_params=pltpu.CompilerParams(
            dimension_semantics=("arbitrary",), vmem_limit_bytes=VMEM_LIMIT),
        name="finish",
    )(x, att, pool, p, w_o, g_post_mix, g_pre_ffn, g_post_ffn, w_gate, w_up, w_down,
      w_ple_gate, w_ple_proj)


def _query_rows(q, nb, t_new):
    qr = q.reshape(nb, t_new, N_HEADS, 2, HEAD_DIM).transpose(0, 2, 3, 1, 4)
    eye = jnp.eye(2, dtype=q.dtype)
    w = qr[:, :, :, :, None, :] * eye[None, None, :, None, :, None]
    return w.reshape(nb, N_HEADS * 2 * t_new, V_DIM)


def kernel(x_prompt, x_sample, p_prompt, p_sample, cache_k, cache_v, state_pool, page_table, w_in, lambda_params, g_sub, w_pool, pool_scale, w_o, g_pre_mix, g_post_mix, g_pre_ffn, g_post_ffn, w_gate, w_up, w_down, w_ple_gate, w_ple_proj):
    batch, seq, _ = x_prompt.shape
    nb, t_new, _ = x_sample.shape
    depth = w_in.shape[0]
    n_pages = page_table.shape[1]
    past_len = n_pages * PAGE_SIZE
    assert seq % ROW_TILE == 0 and seq % Q_TILE == 0 and n_pages % PAGES_PER_STEP == 0
    assert ROW_TILE % KV_TILE == 0 and Q_TILE % KV_TILE == 0 and N_HEADS % HEADS_PER_STEP == 0
    assert t_new & (t_new - 1) == 0 and N_HEADS & (N_HEADS - 1) == 0

    tabs_p = _rope_tables(jnp.arange(seq))
    pos_s = past_len + jnp.arange(t_new)
    tabs_s = tuple(jnp.tile(a, (nb, 1)) for a in _rope_tables(pos_s))
    ck = cache_k.reshape(cache_k.shape[0], cache_k.shape[1], PAGE_ROWS, V_DIM)
    cv = cache_v.reshape(cache_v.shape[0], cache_v.shape[1], PAGE_ROWS, V_DIM)
    bf = lambda a: a.astype(BF16)
    row2 = lambda a: a.reshape(1, -1)

    xp = x_prompt.reshape(batch * seq, D_MODEL)
    xs = x_sample.reshape(nb * t_new, D_MODEL)
    kp_rows, vp_rows, pp_rows, ks_rows, vs_rows, ps_rows = [], [], [], [], [], []
    for i in range(depth):
        lam_init = 0.8 - 0.6 * math.exp(-0.3 * i)
        w_in_i, w_pool_i = bf(w_in[i]), bf(w_pool[i])
        ffn_w = (bf(w_o[i]), row2(g_post_mix[i]), row2(g_pre_ffn[i]), row2(g_post_ffn[i]),
                 bf(w_gate[i]), bf(w_up[i]), bf(w_down[i]), bf(w_ple_gate[i]), bf(w_ple_proj[i]))
        g_in, gsub_i, scale_i = row2(g_pre_mix[i]), row2(g_sub[i]), row2(pool_scale[i])
        lp = lambda_params[i]

        q, kf, vf, u, kb, vt = _proj(xp, g_in, w_in_i, tabs_p, ROW_TILE, seq // ROW_TILE, True)
        att = _attn_prompt(q, kb, vt, lp, gsub_i, lam_init, batch, seq)
        pool = _pool_prompt(u, w_pool_i, scale_i, batch, seq)
        xp = _finish(xp, att, pool, p_prompt[i].reshape(batch * seq, PLE_DIM), *ffn_w, ROW_TILE)
        kp_rows.append(kf.reshape(batch, seq, N_HEADS, V_DIM))
        vp_rows.append(vf.reshape(batch, seq, N_HEADS, V_DIM))
        pp_rows.append(u.reshape(batch, seq, POOL_WIDTH)[:, seq - POOL_HIST:])

        q, kf, vf, u = _proj(xs, g_in, w_in_i, tabs_s, nb * t_new, 1, False)
        new_rows = lambda a: bf(a.reshape(nb, t_new * N_HEADS, V_DIM))
        att = _attn_sample(page_table, lp, gsub_i, _query_rows(q, nb, t_new), new_rows(kf),
                           new_rows(vf), ck, cv, i, lam_init, t_new)
        u3 = u.reshape(nb, t_new, POOL_WIDTH)
        pool = _pool_sample(state_pool[i].transpose(1, 0, 2), u3.transpose(1, 0, 2), w_pool_i,
                            scale_i, past_len)
        pool = pool.transpose(1, 0, 2).reshape(nb * t_new, POOL_WIDTH)
        xs = _finish(xs, att.reshape(nb * t_new, ATT_WIDTH), pool,
                     p_sample[i].reshape(nb * t_new, PLE_DIM), *ffn_w, nb * t_new)
        ks_rows.append(kf.reshape(nb, t_new, N_HEADS, V_DIM))
        vs_rows.append(vf.reshape(nb, t_new, N_HEADS, V_DIM))
        ps_rows.append(jnp.concatenate([state_pool[i], u3], axis=1)[:, t_new:])

    return (xp.reshape(batch, seq, D_MODEL), xs.reshape(nb, t_new, D_MODEL),
            jnp.stack(kp_rows), jnp.stack(vp_rows), jnp.stack(pp_rows),
            jnp.stack(ks_rows), jnp.stack(vs_rows), jnp.stack(ps_rows))
```

```python
import functools
import math

import jax
import jax.numpy as jnp
from jax import lax
from jax.experimental import pallas as pl
from jax.experimental.pallas import tpu as pltpu

D_MODEL = 1024
N_HEADS = 4
HEAD_DIM = 64
V_DIM = 2 * HEAD_DIM
ATT_WIDTH = N_HEADS * V_DIM
QK_WIDTH = N_HEADS * 2 * HEAD_DIM
ROT_DIM = HEAD_DIM // 4
ROT_HALF = ROT_DIM // 2
ROPE_THETA = 500000.0
ATT_SCALE = HEAD_DIM ** -0.5
Q_SCALE = ATT_SCALE * math.log2(math.e)
POOL_WINDOWS = (2, 4, 8, 16)
N_POOL_GROUPS = len(POOL_WINDOWS)
POOL_WIDTH = D_MODEL - ATT_WIDTH
POOL_GROUP = POOL_WIDTH // N_POOL_GROUPS
POOL_HIST = max(POOL_WINDOWS) - 1
HIST_PAD = POOL_HIST + 1
D_FF = 2816
MXU_DIM = 256
FF_SPLIT = (D_FF // MXU_DIM // 2) * MXU_DIM
PLE_DIM = 256
EPS = 1e-6
NEG = -1e30
PAGE_SIZE = 128
PAGE_ROWS = PAGE_SIZE * N_HEADS

LANES = 128
ROW_TILE = 512
Q_TILE = 512
KV_TILE = 512
ONES_ROWS = 16
VT_ROWS = V_DIM + ONES_ROWS
HEADS_PER_STEP = 4
PAGES_PER_STEP = 16
VMEM_LIMIT = 56 * 1024 * 1024

F32 = jnp.float32
BF16 = jnp.bfloat16


def _dot(a, b):
    return jnp.dot(a, b, preferred_element_type=F32)


def _dot_nt(a, b):
    return lax.dot_general(a, b, (((1,), (1,)), ((), ())), preferred_element_type=F32)


def _rms(v, g):
    return v * lax.rsqrt(jnp.mean(v * v, axis=-1, keepdims=True) + EPS) * g


def _lam(lp_ref, lam_init):
    lp = lp_ref[...]
    s1 = jnp.sum(lp[0:1, :] * lp[1:2, :], axis=1, keepdims=True)
    s2 = jnp.sum(lp[2:3, :] * lp[3:4, :], axis=1, keepdims=True)
    return jnp.exp(s1) - jnp.exp(s2) + lam_init


def _subln(o, gsub, lam_init):
    y = o * lax.rsqrt(jnp.mean(o * o, axis=-1, keepdims=True) + EPS)
    return y * gsub * (1.0 - lam_init)


def _proj_kernel(x_ref, g_ref, w_ref, ta_ref, tb_ref, tc_ref,
                 q_ref, kf_ref, vf_ref, u_ref, *attn_refs, tm):
    h = _rms(x_ref[...], g_ref[...]).astype(BF16)
    ta, tb, tc = ta_ref[...], tb_ref[...], tc_ref[...]

    def rope(z):
        return z * ta + pltpu.roll(z, LANES - ROT_HALF, 1) * tb + pltpu.roll(z, ROT_HALF, 1) * tc

    def head_rows(hh):
        return pl.ds(hh, tm, stride=N_HEADS)

    zq = _dot(h, w_ref[:, 0:QK_WIDTH])
    for hh in range(N_HEADS):
        sl = slice(hh * V_DIM, (hh + 1) * V_DIM)
        q_ref[:, sl] = (rope(zq[:, sl]) * Q_SCALE).astype(BF16)
    zk = _dot(h, w_ref[:, QK_WIDTH:2 * QK_WIDTH])
    for hh in range(N_HEADS):
        sl = slice(hh * V_DIM, (hh + 1) * V_DIM)
        k = rope(zk[:, sl])
        kf_ref[head_rows(hh), :] = k
        if attn_refs:
            attn_refs[0][:, sl] = k.astype(BF16)
    zv = _dot(h, w_ref[:, 2 * QK_WIDTH:2 * QK_WIDTH + ATT_WIDTH])
    for hh in range(N_HEADS):
        vf_ref[head_rows(hh), :] = zv[:, hh * V_DIM:(hh + 1) * V_DIM]
    if attn_refs:
        vt_ref = attn_refs[1]
        ones = jnp.ones((ONES_ROWS, KV_TILE), BF16)
        for c in range(tm // KV_TILE):
            vt = zv[c * KV_TILE:(c + 1) * KV_TILE, :].T.astype(BF16)
            for hh in range(N_HEADS):
                vt_ref[c, hh * VT_ROWS:hh * VT_ROWS + V_DIM, :] = vt[hh * V_DIM:(hh + 1) * V_DIM]
                vt_ref[c, hh * VT_ROWS + V_DIM:(hh + 1) * VT_ROWS, :] = ones
    u_ref[...] = _dot(h, w_ref[:, 2 * QK_WIDTH + ATT_WIDTH:])


def _proj(x, g, w_in, tabs, tm, tiles_per_seq, for_prompt):
    m = x.shape[0]
    row = lambda i: (i, 0)
    const = lambda i: (0, 0)
    tab = lambda i: (i % tiles_per_seq, 0)
    wide = pl.BlockSpec((tm, ATT_WIDTH), row)
    head_major = pl.BlockSpec((tm * N_HEADS, V_DIM), row)
    rows_f32 = jax.ShapeDtypeStruct((m * N_HEADS, V_DIM), F32)
    out_specs = [wide, head_major, head_major, wide]
    out_shape = [jax.ShapeDtypeStruct((m, QK_WIDTH), BF16), rows_f32, rows_f32,
                 jax.ShapeDtypeStruct((m, POOL_WIDTH), F32)]
    if for_prompt:
        per = tm // KV_TILE
        out_specs += [wide, pl.BlockSpec((per, N_HEADS * VT_ROWS, KV_TILE), lambda i: (i, 0, 0))]
        out_shape += [jax.ShapeDtypeStruct((m, QK_WIDTH), BF16),
                      jax.ShapeDtypeStruct((m // KV_TILE, N_HEADS * VT_ROWS, KV_TILE), BF16)]
    return pl.pallas_call(
        functools.partial(_proj_kernel, tm=tm),
        grid=(m // tm,),
        in_specs=[
            pl.BlockSpec((tm, D_MODEL), row),
            pl.BlockSpec((1, D_MODEL), const),
            pl.BlockSpec(w_in.shape, const),
            pl.BlockSpec((tm, LANES), tab),
            pl.BlockSpec((tm, LANES), tab),
            pl.BlockSpec((tm, LANES), tab),
        ],
        out_specs=out_specs,
        out_shape=out_shape,
        compiler_params=pltpu.CompilerParams(
            dimension_semantics=("arbitrary",), vmem_limit_bytes=VMEM_LIMIT),
        name="proj",
    )(x, g, w_in, *tabs)


def _rope_tables(pos):
    t = pos.shape[0]
    inv = jnp.power(ROPE_THETA, -jnp.arange(0, ROT_DIM, 2, dtype=F32) / ROT_DIM)
    ang = pos.astype(F32)[:, None] * inv[None, :]
    cos, sin = jnp.cos(ang), jnp.sin(ang)
    rest = HEAD_DIM - ROT_DIM
    ta = jnp.concatenate([cos, cos, jnp.ones((t, rest), F32)], axis=-1)
    tb = jnp.concatenate([-sin, jnp.zeros((t, rest + ROT_HALF), F32)], axis=-1)
    tc = jnp.concatenate([jnp.zeros((t, ROT_HALF), F32), sin, jnp.zeros((t, rest), F32)], axis=-1)
    return tuple(jnp.concatenate([a, a], axis=-1) for a in (ta, tb, tc))


def _attn_prompt_kernel(lp_ref, gsub_ref, q_ref, k_ref, vt_ref, o_ref,
                        qst_ref, m_ref, acc_ref, *, lam_init, tq, tk, nh):
    qi = pl.program_id(2)
    comp = lax.broadcasted_iota(jnp.int32, (V_DIM, tq), 0) < HEAD_DIM
    for hh in range(nh):
        qt = q_ref[:, hh * V_DIM:(hh + 1) * V_DIM].astype(F32).T
        qst_ref[hh, :, 0:tq] = jnp.where(comp, qt, 0.0).astype(BF16)
        qst_ref[hh, :, tq:] = jnp.where(comp, 0.0, qt).astype(BF16)
    m_ref[...] = jnp.full(m_ref.shape, NEG, F32)
    acc_ref[...] = jnp.zeros(acc_ref.shape, F32)

    def step(j, mask):
        start = pl.multiple_of(j * tk, tk)
        scores = []
        for hh in range(nh):
            s = _dot(k_ref[pl.ds(start, tk), hh * V_DIM:(hh + 1) * V_DIM], qst_ref[hh])
            scores.append(s if mask is None else jnp.where(mask, s, NEG))
        probs = []
        for hh in range(nh):
            m_old = m_ref[hh]
            m_new = jnp.maximum(m_old, jnp.max(scores[hh], axis=0, keepdims=True))
            m_ref[hh] = m_new
            probs.append((jnp.exp2(m_old - m_new), jnp.exp2(scores[hh] - m_new).astype(BF16)))
        for hh in range(nh):
            alpha, p = probs[hh]
            acc_ref[hh] = alpha * acc_ref[hh] + _dot(vt_ref[j, hh * VT_ROWS:(hh + 1) * VT_ROWS, :], p)

    per = tq // tk

    def body(j, carry):
        step(j, None)
        return carry

    lax.fori_loop(0, qi * per, body, 0)
    row = lax.broadcasted_iota(jnp.int32, (tk, 2 * tq), 0)
    col = lax.broadcasted_iota(jnp.int32, (tk, 2 * tq), 1)
    for d in range(per):
        step(qi * per + d, (d * tk + row) <= (col & (tq - 1)))

    lam = _lam(lp_ref, lam_init)
    for hh in range(nh):
        acc = acc_ref[hh, 0:V_DIM, :]
        l = acc_ref[hh, V_DIM:V_DIM + 1, :]
        ot = acc[:, 0:tq] / l[:, 0:tq] - lam * (acc[:, tq:] / l[:, tq:])
        o_ref[:, hh * V_DIM:(hh + 1) * V_DIM] = _subln(ot.T, gsub_ref[...], lam_init).astype(o_ref.dtype)


def _attn_prompt(q, k, vt, lp, gsub, lam_init, batch, seq):
    tq, tk, nh = Q_TILE, KV_TILE, HEADS_PER_STEP
    nq = seq // tq
    kern = functools.partial(_attn_prompt_kernel, lam_init=lam_init, tq=tq, tk=tk, nh=nh)
    q_spec = pl.BlockSpec((tq, nh * V_DIM), lambda b, h, i: (b * nq + i, h))
    return pl.pallas_call(
        kern,
        grid=(batch, N_HEADS // nh, nq),
        in_specs=[
            pl.BlockSpec(lp.shape, lambda b, h, i: (0, 0)),
            pl.BlockSpec((1, V_DIM), lambda b, h, i: (0, 0)),
            q_spec,
            pl.BlockSpec((seq, nh * V_DIM), lambda b, h, i: (b, h)),
            pl.BlockSpec((seq // tk, nh * VT_ROWS, tk), lambda b, h, i: (b, h, 0)),
        ],
        out_specs=q_spec,
        out_shape=jax.ShapeDtypeStruct((batch * seq, ATT_WIDTH), BF16),
        scratch_shapes=[
            pltpu.VMEM((nh, V_DIM, 2 * tq), BF16),
            pltpu.VMEM((nh, 1, 2 * tq), F32),
            pltpu.VMEM((nh, VT_ROWS, 2 * tq), F32),
        ],
        compiler_params=pltpu.CompilerParams(
            dimension_semantics=("arbitrary", "arbitrary", "arbitrary"),
            vmem_limit_bytes=VMEM_LIMIT),
        name="attn_prompt",
    )(lp, gsub, q, k, vt)


def _pool_prompt_kernel(u_ref, prev_ref, w_ref, sc_ref, o_ref, ext_ref, *, tm):
    i = pl.program_id(1)
    prev = prev_ref[...]
    ext_ref[0:HIST_PAD, :] = jnp.where(i == 0, jnp.zeros_like(prev), prev)
    ext_ref[HIST_PAD:, :] = u_ref[...]
    pos = i * tm + lax.broadcasted_iota(jnp.int32, (tm, POOL_GROUP), 0)
    for g, w in enumerate(POOL_WINDOWS):
        sl = slice(g * POOL_GROUP, (g + 1) * POOL_GROUP)
        cur = ext_ref[HIST_PAD:, sl]
        tot = cur
        for j in range(1, w):
            tot = tot + ext_ref[HIST_PAD - j:HIST_PAD - j + tm, sl]
        cnt = jnp.minimum(pos + 1, w).astype(F32)
        pooled = (tot / cnt - cur).astype(BF16)
        o_ref[:, sl] = (_dot(pooled, w_ref[g]) * sc_ref[:, sl]).astype(o_ref.dtype)


def _pool_prompt(u, w_pool, scale, batch, seq):
    tm = ROW_TILE
    nt = seq // tm
    per = tm // HIST_PAD
    kern = functools.partial(_pool_prompt_kernel, tm=tm)
    return pl.pallas_call(
        kern,
        grid=(batch, nt),
        in_specs=[
            pl.BlockSpec((tm, POOL_WIDTH), lambda b, i: (b * nt + i, 0)),
            pl.BlockSpec((HIST_PAD, POOL_WIDTH),
                         lambda b, i: (jnp.maximum((b * nt + i) * per - 1, 0), 0)),
            pl.BlockSpec(w_pool.shape, lambda b, i: (0, 0, 0)),
            pl.BlockSpec((1, POOL_WIDTH), lambda b, i: (0, 0)),
        ],
        out_specs=pl.BlockSpec((tm, POOL_WIDTH), lambda b, i: (b * nt + i, 0)),
        out_shape=jax.ShapeDtypeStruct((batch * seq, POOL_WIDTH), BF16),
        scratch_shapes=[pltpu.VMEM((tm + HIST_PAD, POOL_WIDTH), F32)],
        compiler_params=pltpu.CompilerParams(
            dimension_semantics=("arbitrary", "arbitrary"), vmem_limit_bytes=VMEM_LIMIT),
        name="pool_prompt",
    )(u, u, w_pool, scale)


def _pool_sample_kernel(hist_ref, u_ref, w_ref, sc_ref, o_ref, *, t_new, pos0):
    ext = [hist_ref[j] for j in range(POOL_HIST)] + [u_ref[t] for t in range(t_new)]
    for g, w in enumerate(POOL_WINDOWS):
        sl = slice(g * POOL_GROUP, (g + 1) * POOL_GROUP)
        rows = []
        for t in range(t_new):
            cur = ext[POOL_HIST + t][:, sl]
            tot = cur
            for j in range(1, w):
                tot = tot + ext[POOL_HIST + t - j][:, sl]
            cnt = float(min(pos0 + t + 1, w))
            rows.append(tot / cnt - cur)
        pooled = jnp.concatenate(rows, axis=0).astype(BF16)
        y = _dot(pooled, w_ref[g]) * sc_ref[:, sl]
        nb = y.shape[0] // t_new
        for t in range(t_new):
            o_ref[t, :, sl] = y[t * nb:(t + 1) * nb]


def _pool_sample(hist_tm, u_tm, w_pool, scale, pos0):
    t_new, nb, _ = u_tm.shape
    kern = functools.partial(_pool_sample_kernel, t_new=t_new, pos0=pos0)
    return pl.pallas_call(
        kern,
        out_shape=jax.ShapeDtypeStruct((t_new, nb, POOL_WIDTH), F32),
        name="pool_sample",
    )(hist_tm, u_tm, w_pool, scale)


def _attn_sample_kernel(pt_ref, lp_ref, gsub_ref, w_ref, kn_ref, vn_ref, ck_hbm, cv_hbm, o_ref,
                        kbuf, vbuf, sem, *, layer, lam_init, t_new, n_seq, n_groups):
    g_pages = PAGES_PER_STEP
    b = pl.program_id(0)
    w = w_ref[0]
    head_shift = (2 * t_new).bit_length() - 1
    tok_shift = N_HEADS.bit_length() - 1

    def own_head(shape):
        row = lax.broadcasted_iota(jnp.int32, shape, 0)
        col = lax.broadcasted_iota(jnp.int32, shape, 1)
        return (col & (N_HEADS - 1)) == (row >> head_shift), row, col

    def group_copies(seq, grp, slot):
        out = []
        for j in range(g_pages):
            page = pt_ref[seq, grp * g_pages + j]
            rows = pl.ds(j * PAGE_ROWS, PAGE_ROWS)
            out.append(pltpu.make_async_copy(ck_hbm.at[layer, page], kbuf.at[slot, rows], sem.at[slot, 0]))
            out.append(pltpu.make_async_copy(cv_hbm.at[layer, page], vbuf.at[slot, rows], sem.at[slot, 1]))
        return out

    @pl.when(b == 0)
    def _first():
        for c in group_copies(0, 0, 0):
            c.start()

    sn = _dot_nt(w, kn_ref[0])
    own, row, col = own_head(sn.shape)
    valid = own & ((col >> tok_shift) <= (row & (t_new - 1)))
    sn = jnp.where(valid, sn, NEG)
    m0 = jnp.max(sn, axis=1, keepdims=True)
    p0 = jnp.where(valid, jnp.exp2(sn - m0), 0.0)
    l0 = jnp.sum(p0, axis=1, keepdims=True)
    acc0 = _dot(p0.astype(BF16), vn_ref[0])

    def group(g, carry):
        m_old, l_old, acc_old = carry
        slot = g & 1
        last = g + 1 == n_groups
        nxt_seq = jnp.where(last, b + 1, b)
        nxt_grp = jnp.where(last, 0, g + 1)

        @pl.when(nxt_seq < n_seq)
        def _prefetch():
            for c in group_copies(nxt_seq, nxt_grp, 1 - slot):
                c.start()

        for c in group_copies(b, g, slot):
            c.wait()

        bias = jnp.where(own_head((w.shape[0], PAGE_ROWS))[0], 0.0, NEG)
        scores = []
        for j in range(g_pages):
            kj = kbuf[slot, j * PAGE_ROWS:(j + 1) * PAGE_ROWS, :].astype(BF16)
            scores.append(_dot_nt(w, kj) + bias)
        top = scores[0]
        for s in scores[1:]:
            top = jnp.maximum(top, s)
        m_new = jnp.maximum(m_old, jnp.max(top, axis=1, keepdims=True))
        alpha = jnp.exp2(m_old - m_new)
        tot = None
        acc = alpha * acc_old
        for j in range(g_pages):
            p = jnp.exp2(scores[j] - m_new)
            tot = p if tot is None else tot + p
            vj = vbuf[slot, j * PAGE_ROWS:(j + 1) * PAGE_ROWS, :].astype(BF16)
            acc = acc + _dot(p.astype(BF16), vj)
        l_new = alpha * l_old + jnp.sum(tot, axis=1, keepdims=True)
        return m_new, l_new, acc

    _, l, acc = lax.fori_loop(0, n_groups, group, (m0, l0, acc0))

    lam = _lam(lp_ref, lam_init)
    accn = acc / l
    per = 2 * t_new
    for hh in range(N_HEADS):
        blk = accn[hh * per:(hh + 1) * per]
        o = blk[0:t_new] - lam * blk[t_new:per]
        o_ref[0, :, hh * V_DIM:(hh + 1) * V_DIM] = _subln(o, gsub_ref[...], lam_init)


def _attn_sample(page_table, lp, gsub, w_blk, k_new, v_new, cache_k, cache_v, layer, lam_init,
                 t_new):
    nb, n_pages = page_table.shape
    g_pages = PAGES_PER_STEP
    n_groups = n_pages // g_pages
    n_rows = w_blk.shape[1]
    new_rows = k_new.shape[1]
    assert n_groups % 2 == 0
    kern = functools.partial(_attn_sample_kernel, layer=layer, lam_init=lam_init, t_new=t_new,
                             n_seq=nb, n_groups=n_groups)
    per_seq = lambda b, pt: (b, 0, 0)
    grid_spec = pltpu.PrefetchScalarGridSpec(
        num_scalar_prefetch=1,
        grid=(nb,),
        in_specs=[
            pl.BlockSpec(lp.shape, lambda b, pt: (0, 0)),
            pl.BlockSpec((1, V_DIM), lambda b, pt: (0, 0)),
            pl.BlockSpec((1, n_rows, V_DIM), per_seq),
            pl.BlockSpec((1, new_rows, V_DIM), per_seq),
            pl.BlockSpec((1, new_rows, V_DIM), per_seq),
            pl.BlockSpec(memory_space=pl.ANY),
            pl.BlockSpec(memory_space=pl.ANY),
        ],
        out_specs=pl.BlockSpec((1, t_new, ATT_WIDTH), per_seq),
        scratch_shapes=[
            pltpu.VMEM((2, g_pages * PAGE_ROWS, V_DIM), F32),
            pltpu.VMEM((2, g_pages * PAGE_ROWS, V_DIM), F32),
            pltpu.SemaphoreType.DMA((2, 2)),
        ],
    )
    return pl.pallas_call(
        kern,
        grid_spec=grid_spec,
        out_shape=jax.ShapeDtypeStruct((nb, t_new, ATT_WIDTH), F32),
        compiler_params=pltpu.CompilerParams(
            dimension_semantics=("arbitrary",), vmem_limit_bytes=VMEM_LIMIT),
        name="attn_sample",
    )(page_table, lp, gsub, w_blk, k_new, v_new, cache_k, cache_v)


def _finish_kernel(x_ref, att_ref, pool_ref, p_ref, wo_ref, gpm_ref, gpf_ref, gqf_ref,
                   wg_ref, wu_ref, wd_ref, wpg_ref, wpp_ref, o_ref):
    mix = (_dot(att_ref[...].astype(BF16), wo_ref[0:ATT_WIDTH, :])
           + _dot(pool_ref[...].astype(BF16), wo_ref[ATT_WIDTH:, :]))
    x1 = x_ref[...] + _rms(mix, gpm_ref[...])
    h = _rms(x1, gpf_ref[...]).astype(BF16)
    f = None
    for sl in (slice(0, FF_SPLIT), slice(FF_SPLIT, D_FF)):
        a = (jax.nn.silu(_dot(h, wg_ref[:, sl])) * _dot(h, wu_ref[:, sl])).astype(BF16)
        part = _dot(a, wd_ref[sl, :])
        f = part if f is None else f + part
    x2 = x1 + _rms(f, gqf_ref[...])
    gate = jax.nn.sigmoid(_dot(x2.astype(BF16), wpg_ref[...]))
    o_ref[...] = x2 + gate * _dot(p_ref[...].astype(BF16), wpp_ref[...])


def _finish(x, att, pool, p, w_o, g_post_mix, g_pre_ffn, g_post_ffn, w_gate, w_up, w_down,
            w_ple_gate, w_ple_proj, tm):
    m = x.shape[0]
    row = lambda i: (i, 0)
    const = lambda i: (0, 0)
    resident = lambda a: pl.BlockSpec(a.shape, const, pipeline_mode=pl.Buffered(1))
    return pl.pallas_call(
        _finish_kernel,
        grid=(m // tm,),
        in_specs=[
            pl.BlockSpec((tm, D_MODEL), row),
            pl.BlockSpec((tm, ATT_WIDTH), row),
            pl.BlockSpec((tm, POOL_WIDTH), row),
            pl.BlockSpec((tm, PLE_DIM), row),
            resident(w_o), resident(g_post_mix), resident(g_pre_ffn), resident(g_post_ffn),
            resident(w_gate), resident(w_up), resident(w_down),
            resident(w_ple_gate), resident(w_ple_proj),
        ],
        out_specs=pl.BlockSpec((tm, D_MODEL), row),
        out_shape=jax.ShapeDtypeStruct((m, D_MODEL), F32),
        compiler_params=pltpu.CompilerParams(
            dimension_semantics=("arbitrary",), vmem_limit_bytes=VMEM_LIMIT),
        name="finish",
    )(x, att, pool, p, w_o, g_post_mix, g_pre_ffn, g_post_ffn, w_gate, w_up, w_down,
      w_ple_gate, w_ple_proj)


def _query_rows(q, nb, t_new):
    qr = q.reshape(nb, t_new, N_HEADS, 2, HEAD_DIM).transpose(0, 2, 3, 1, 4)
    eye = jnp.eye(2, dtype=q.dtype)
    w = qr[:, :, :, :, None, :] * eye[None, None, :, None, :, None]
    return w.reshape(nb, N_HEADS * 2 * t_new, V_DIM)


def kernel(x_prompt, x_sample, p_prompt, p_sample, cache_k, cache_v, state_pool, page_table, w_in, lambda_params, g_sub, w_pool, pool_scale, w_o, g_pre_mix, g_post_mix, g_pre_ffn, g_post_ffn, w_gate, w_up, w_down, w_ple_gate, w_ple_proj):
    batch, seq, _ = x_prompt.shape
    nb, t_new, _ = x_sample.shape
    depth = w_in.shape[0]
    n_pages = page_table.shape[1]
    past_len = n_pages * PAGE_SIZE
    assert seq % ROW_TILE == 0 and seq % Q_TILE == 0 and n_pages % PAGES_PER_STEP == 0
    assert ROW_TILE % KV_TILE == 0 and Q_TILE % KV_TILE == 0 and N_HEADS % HEADS_PER_STEP == 0
    assert t_new & (t_new - 1) == 0 and N_HEADS & (N_HEADS - 1) == 0

    tabs_p = _rope_tables(jnp.arange(seq))
    pos_s = past_len + jnp.arange(t_new)
    tabs_s = tuple(jnp.tile(a, (nb, 1)) for a in _rope_tables(pos_s))
    ck = cache_k.reshape(cache_k.shape[0], cache_k.shape[1], PAGE_ROWS, V_DIM)
    cv = cache_v.reshape(cache_v.shape[0], cache_v.shape[1], PAGE_ROWS, V_DIM)
    bf = lambda a: a.astype(BF16)
    row2 = lambda a: a.reshape(1, -1)

    xp = x_prompt.reshape(batch * seq, D_MODEL)
    xs = x_sample.reshape(nb * t_new, D_MODEL)
    kp_rows, vp_rows, pp_rows, ks_rows, vs_rows, ps_rows = [], [], [], [], [], []
    for i in range(depth):
        lam_init = 0.8 - 0.6 * math.exp(-0.3 * i)
        w_in_i, w_pool_i = bf(w_in[i]), bf(w_pool[i])
        ffn_w = (bf(w_o[i]), row2(g_post_mix[i]), row2(g_pre_ffn[i]), row2(g_post_ffn[i]),
                 bf(w_gate[i]), bf(w_up[i]), bf(w_down[i]), bf(w_ple_gate[i]), bf(w_ple_proj[i]))
        g_in, gsub_i, scale_i = row2(g_pre_mix[i]), row2(g_sub[i]), row2(pool_scale[i])
        lp = lambda_params[i]

        q, kf, vf, u, kb, vt = _proj(xp, g_in, w_in_i, tabs_p, ROW_TILE, seq // ROW_TILE, True)
        att = _attn_prompt(q, kb, vt, lp, gsub_i, lam_init, batch, seq)
        pool = _pool_prompt(u, w_pool_i, scale_i, batch, seq)
        xp = _finish(xp, att, pool, p_prompt[i].reshape(batch * seq, PLE_DIM), *ffn_w, ROW_TILE)
        kp_rows.append(kf.reshape(batch, seq, N_HEADS, V_DIM))
        vp_rows.append(vf.reshape(batch, seq, N_HEADS, V_DIM))
        pp_rows.append(u.reshape(batch, seq, POOL_WIDTH)[:, seq - POOL_HIST:])

        q, kf, vf, u = _proj(xs, g_in, w_in_i, tabs_s, nb * t_new, 1, False)
        new_rows = lambda a: bf(a.reshape(nb, t_new * N_HEADS, V_DIM))
        att = _attn_sample(page_table, lp, gsub_i, _query_rows(q, nb, t_new), new_rows(kf),
                           new_rows(vf), ck, cv, i, lam_init, t_new)
        u3 = u.reshape(nb, t_new, POOL_WIDTH)
        pool = _pool_sample(state_pool[i].transpose(1, 0, 2), u3.transpose(1, 0, 2), w_pool_i,
                            scale_i, past_len)
        pool = pool.transpose(1, 0, 2).reshape(nb * t_new, POOL_WIDTH)
        xs = _finish(xs, att.reshape(nb * t_new, ATT_WIDTH), pool,
                     p_sample[i].reshape(nb * t_new, PLE_DIM), *ffn_w, nb * t_new)
        ks_rows.append(kf.reshape(nb, t_new, N_HEADS, V_DIM))
        vs_rows.append(vf.reshape(nb, t_new, N_HEADS, V_DIM))
        ps_rows.append(jnp.concatenate([state_pool[i], u3], axis=1)[:, t_new:])

    return (xp.reshape(batch, seq, D_MODEL), xs.reshape(nb, t_new, D_MODEL),
            jnp.stack(kp_rows), jnp.stack(vp_rows), jnp.stack(pp_rows),
            jnp.stack(ks_rows), jnp.stack(vs_rows), jnp.stack(ps_rows))
```

```python
import functools
import math

import jax
import jax.numpy as jnp
from jax import lax
from jax.experimental import pallas as pl
from jax.experimental.pallas import tpu as pltpu

D_MODEL = 1024
N_HEADS = 4
HEAD_DIM = 64
V_DIM = 2 * HEAD_DIM
ATT_WIDTH = N_HEADS * V_DIM
QK_WIDTH = N_HEADS * 2 * HEAD_DIM
ROT_DIM = HEAD_DIM // 4
ROT_HALF = ROT_DIM // 2
ROPE_THETA = 500000.0
ATT_SCALE = HEAD_DIM ** -0.5
Q_SCALE = ATT_SCALE * math.log2(math.e)
POOL_WINDOWS = (2, 4, 8, 16)
N_POOL_GROUPS = len(POOL_WINDOWS)
POOL_WIDTH = D_MODEL - ATT_WIDTH
POOL_GROUP = POOL_WIDTH // N_POOL_GROUPS
POOL_HIST = max(POOL_WINDOWS) - 1
HIST_PAD = POOL_HIST + 1
D_FF = 2816
MXU_DIM = 256
FF_SPLIT = (D_FF // MXU_DIM // 2) * MXU_DIM
PLE_DIM = 256
EPS = 1e-6
NEG = -1e30
PAGE_SIZE = 128
PAGE_ROWS = PAGE_SIZE * N_HEADS

LANES = 128
ROW_TILE = 512
Q_TILE = 512
KV_TILE = 512
ONES_ROWS = 16
VT_ROWS = V_DIM + ONES_ROWS
HEADS_PER_STEP = 4
PAGES_PER_STEP = 16
VMEM_LIMIT = 56 * 1024 * 1024

F32 = jnp.float32
BF16 = jnp.bfloat16


def _dot(a, b):
    return jnp.dot(a, b, preferred_element_type=F32)


def _dot_nt(a, b):
    return lax.dot_general(a, b, (((1,), (1,)), ((), ())), preferred_element_type=F32)


def _rms(v, g):
    return v * lax.rsqrt(jnp.mean(v * v, axis=-1, keepdims=True) + EPS) * g


def _lam(lp_ref, lam_init):
    lp = lp_ref[...]
    s1 = jnp.sum(lp[0:1, :] * lp[1:2, :], axis=1, keepdims=True)
    s2 = jnp.sum(lp[2:3, :] * lp[3:4, :], axis=1, keepdims=True)
    return jnp.exp(s1) - jnp.exp(s2) + lam_init


def _subln(o, gsub, lam_init):
    y = o * lax.rsqrt(jnp.mean(o * o, axis=-1, keepdims=True) + EPS)
    return y * gsub * (1.0 - lam_init)


def _proj_kernel(*refs, tm, tiles_per_seq, for_prompt):
    if for_prompt:
        (x_ref, g_ref, w_ref, ta_ref, tb_ref, tc_ref, wp_ref, sc_ref,
         q_ref, kf_ref, vf_ref, kb_ref, vt_ref, pool_ref, tail_ref, ext_ref) = refs
    else:
        x_ref, g_ref, w_ref, ta_ref, tb_ref, tc_ref, q_ref, kf_ref, vf_ref, u_ref = refs
    h = _rms(x_ref[...], g_ref[...]).astype(BF16)
    ta, tb, tc = ta_ref[...], tb_ref[...], tc_ref[...]

    def rope(z):
        return z * ta + pltpu.roll(z, LANES - ROT_HALF, 1) * tb + pltpu.roll(z, ROT_HALF, 1) * tc

    def head_rows(hh):
        return pl.ds(hh, tm, stride=N_HEADS)

    zq = _dot(h, w_ref[:, 0:QK_WIDTH])
    for hh in range(N_HEADS):
        sl = slice(hh * V_DIM, (hh + 1) * V_DIM)
        q_ref[:, sl] = (rope(zq[:, sl]) * Q_SCALE).astype(BF16)
    zk = _dot(h, w_ref[:, QK_WIDTH:2 * QK_WIDTH])
    for hh in range(N_HEADS):
        sl = slice(hh * V_DIM, (hh + 1) * V_DIM)
        k = rope(zk[:, sl])
        kf_ref[head_rows(hh), :] = k
        if for_prompt:
            kb_ref[:, sl] = k.astype(BF16)
    zv = _dot(h, w_ref[:, 2 * QK_WIDTH:2 * QK_WIDTH + ATT_WIDTH])
    for hh in range(N_HEADS):
        vf_ref[head_rows(hh), :] = zv[:, hh * V_DIM:(hh + 1) * V_DIM]
    u = _dot(h, w_ref[:, 2 * QK_WIDTH + ATT_WIDTH:])
    if not for_prompt:
        u_ref[...] = u
    else:
        ones = jnp.ones((ONES_ROWS, KV_TILE), BF16)
        for c in range(tm // KV_TILE):
            vt = zv[c * KV_TILE:(c + 1) * KV_TILE, :].T.astype(BF16)
            for hh in range(N_HEADS):
                vt_ref[c, hh * VT_ROWS:hh * VT_ROWS + V_DIM, :] = vt[hh * V_DIM:(hh + 1) * V_DIM]
                vt_ref[c, hh * VT_ROWS + V_DIM:(hh + 1) * VT_ROWS, :] = ones
        _pool_rows(u, ext_ref, wp_ref, sc_ref, pool_ref, tail_ref, tm, tiles_per_seq)


def _pool_rows(u, ext_ref, w_ref, sc_ref, o_ref, tail_ref, tm, tiles_per_seq):
    it = pl.program_id(0) % tiles_per_seq

    @pl.when(it == 0)
    def _zero_history():
        ext_ref[0:HIST_PAD, :] = jnp.zeros((HIST_PAD, POOL_WIDTH), F32)

    @pl.when(it != 0)
    def _carry_history():
        ext_ref[0:HIST_PAD, :] = ext_ref[tm:tm + HIST_PAD, :]

    ext_ref[HIST_PAD:, :] = u
    tail_ref[...] = u[tm - HIST_PAD:, :]
    pos = it * tm + lax.broadcasted_iota(jnp.int32, (tm, POOL_GROUP), 0)
    for g, w in enumerate(POOL_WINDOWS):
        sl = slice(g * POOL_GROUP, (g + 1) * POOL_GROUP)
        cur = u[:, sl]
        tot = cur
        for j in range(1, w):
            tot = tot + ext_ref[HIST_PAD - j:HIST_PAD - j + tm, sl]
        cnt = jnp.minimum(pos + 1, w).astype(F32)
        pooled = (tot / cnt - cur).astype(BF16)
        o_ref[:, sl] = (_dot(pooled, w_ref[g]) * sc_ref[:, sl]).astype(o_ref.dtype)


def _layer_spec(a, layer, index_map_tail, **kw):
    return pl.BlockSpec((None,) + a.shape[1:], lambda *_: (layer,) + index_map_tail, **kw)


def _proj(x, g, w_in, tabs, layer, tm, tiles_per_seq, pool_w=None):
    m = x.shape[0]
    for_prompt = pool_w is not None
    row = lambda i: (i, 0)
    tab = lambda i: (i % tiles_per_seq, 0)
    wide = pl.BlockSpec((tm, ATT_WIDTH), row)
    head_major = pl.BlockSpec((tm * N_HEADS, V_DIM), row)
    rows_f32 = jax.ShapeDtypeStruct((m * N_HEADS, V_DIM), F32)
    in_specs = [
        pl.BlockSpec((tm, D_MODEL), row),
        _layer_spec(g, layer, (0, 0)),
        _layer_spec(w_in, layer, (0, 0)),
        pl.BlockSpec((tm, LANES), tab),
        pl.BlockSpec((tm, LANES), tab),
        pl.BlockSpec((tm, LANES), tab),
    ]
    args = [x, g, w_in, *tabs]
    out_specs = [wide, head_major, head_major]
    out_shape = [jax.ShapeDtypeStruct((m, QK_WIDTH), BF16), rows_f32, rows_f32]
    scratch = []
    if for_prompt:
        w_pool, scale = pool_w
        in_specs += [_layer_spec(w_pool, layer, (0, 0, 0)), _layer_spec(scale, layer, (0, 0))]
        args += [w_pool, scale]
        per = tm // KV_TILE
        out_specs += [wide,
                      pl.BlockSpec((per, N_HEADS * VT_ROWS, KV_TILE), lambda i: (i, 0, 0)),
                      wide,
                      pl.BlockSpec((HIST_PAD, POOL_WIDTH), lambda i: (i // tiles_per_seq, 0))]
        out_shape += [jax.ShapeDtypeStruct((m, QK_WIDTH), BF16),
                      jax.ShapeDtypeStruct((m // KV_TILE, N_HEADS * VT_ROWS, KV_TILE), BF16),
                      jax.ShapeDtypeStruct((m, POOL_WIDTH), BF16),
                      jax.ShapeDtypeStruct((m // tiles_per_seq // tm * HIST_PAD, POOL_WIDTH), F32)]
        scratch = [pltpu.VMEM((tm + HIST_PAD, POOL_WIDTH), F32)]
    else:
        out_specs += [wide]
        out_shape += [jax.ShapeDtypeStruct((m, POOL_WIDTH), F32)]
    return pl.pallas_call(
        functools.partial(_proj_kernel, tm=tm, tiles_per_seq=tiles_per_seq, for_prompt=for_prompt),
        grid=(m // tm,),
        in_specs=in_specs,
        out_specs=out_specs,
        out_shape=out_shape,
        scratch_shapes=scratch,
        compiler_params=pltpu.CompilerParams(
            dimension_semantics=("arbitrary",), vmem_limit_bytes=VMEM_LIMIT),
        name="proj",
    )(*args)


def _rope_tables(pos):
    lane = jnp.arange(LANES) % HEAD_DIM
    inv = jnp.power(ROPE_THETA, -(2 * (lane % ROT_HALF)).astype(F32) / ROT_DIM)
    ang = pos.astype(F32)[:, None] * inv[None, :]
    cos, sin = jnp.cos(ang), jnp.sin(ang)
    first = (lane < ROT_HALF)[None, :]
    second = ((lane >= ROT_HALF) & (lane < ROT_DIM))[None, :]
    ta = jnp.where(first | second, cos, 1.0)
    tb = jnp.where(first, -sin, 0.0)
    tc = jnp.where(second, sin, 0.0)
    return ta, tb, tc


def _attn_prompt_kernel(lp_ref, gsub_ref, q_ref, k_ref, vt_ref, o_ref,
                        qst_ref, m_ref, acc_ref, *, lam_init, tq, tk, nh):
    qi = pl.program_id(2)
    comp = lax.broadcasted_iota(jnp.int32, (V_DIM, tq), 0) < HEAD_DIM
    for hh in range(nh):
        qt = q_ref[:, hh * V_DIM:(hh + 1) * V_DIM].astype(F32).T
        qst_ref[hh, :, 0:tq] = jnp.where(comp, qt, 0.0).astype(BF16)
        qst_ref[hh, :, tq:] = jnp.where(comp, 0.0, qt).astype(BF16)
    m_ref[...] = jnp.full(m_ref.shape, NEG, F32)
    acc_ref[...] = jnp.zeros(acc_ref.shape, F32)

    def step(j, mask):
        start = pl.multiple_of(j * tk, tk)
        scores = []
        for hh in range(nh):
            s = _dot(k_ref[pl.ds(start, tk), hh * V_DIM:(hh + 1) * V_DIM], qst_ref[hh])
            scores.append(s if mask is None else jnp.where(mask, s, NEG))
        probs = []
        for hh in range(nh):
            m_old = m_ref[hh]
            m_new = jnp.maximum(m_old, jnp.max(scores[hh], axis=0, keepdims=True))
            m_ref[hh] = m_new
            probs.append((jnp.exp2(m_old - m_new), jnp.exp2(scores[hh] - m_new).astype(BF16)))
        for hh in range(nh):
            alpha, p = probs[hh]
            acc_ref[hh] = alpha * acc_ref[hh] + _dot(vt_ref[j, hh * VT_ROWS:(hh + 1) * VT_ROWS, :], p)

    per = tq // tk

    def body(j, carry):
        step(j, None)
        return carry

    lax.fori_loop(0, qi * per, body, 0)
    row = lax.broadcasted_iota(jnp.int32, (tk, 2 * tq), 0)
    col = lax.broadcasted_iota(jnp.int32, (tk, 2 * tq), 1)
    for d in range(per):
        step(qi * per + d, (d * tk + row) <= (col & (tq - 1)))

    lam = _lam(lp_ref, lam_init)
    for hh in range(nh):
        acc = acc_ref[hh, 0:V_DIM, :]
        l = acc_ref[hh, V_DIM:V_DIM + 1, :]
        ot = acc[:, 0:tq] / l[:, 0:tq] - lam * (acc[:, tq:] / l[:, tq:])
        o_ref[:, hh * V_DIM:(hh + 1) * V_DIM] = _subln(ot.T, gsub_ref[...], lam_init).astype(o_ref.dtype)


def _attn_prompt(q, k, vt, lp, gsub, layer, lam_init, batch, seq):
    tq, tk, nh = Q_TILE, KV_TILE, HEADS_PER_STEP
    nq = seq // tq
    kern = functools.partial(_attn_prompt_kernel, lam_init=lam_init, tq=tq, tk=tk, nh=nh)
    q_spec = pl.BlockSpec((tq, nh * V_DIM), lambda b, h, i: (b * nq + i, h))
    return pl.pallas_call(
        kern,
        grid=(batch, N_HEADS // nh, nq),
        in_specs=[
            _layer_spec(lp, layer, (0, 0)),
            _layer_spec(gsub, layer, (0, 0)),
            q_spec,
            pl.BlockSpec((seq, nh * V_DIM), lambda b, h, i: (b, h)),
            pl.BlockSpec((seq // tk, nh * VT_ROWS, tk), lambda b, h, i: (b, h, 0)),
        ],
        out_specs=q_spec,
        out_shape=jax.ShapeDtypeStruct((batch * seq, ATT_WIDTH), BF16),
        scratch_shapes=[
            pltpu.VMEM((nh, V_DIM, 2 * tq), BF16),
            pltpu.VMEM((nh, 1, 2 * tq), F32),
            pltpu.VMEM((nh, VT_ROWS, 2 * tq), F32),
        ],
        compiler_params=pltpu.CompilerParams(
            dimension_semantics=("arbitrary", "arbitrary", "arbitrary"),
            vmem_limit_bytes=VMEM_LIMIT),
        name="attn_prompt",
    )(lp, gsub, q, k, vt)


def _pool_sample_kernel(hist_ref, u_ref, w_ref, sc_ref, o_ref, *, t_new, pos0):
    ext = [hist_ref[j] for j in range(POOL_HIST)] + [u_ref[t] for t in range(t_new)]
    for g, w in enumerate(POOL_WINDOWS):
        sl = slice(g * POOL_GROUP, (g + 1) * POOL_GROUP)
        rows = []
        for t in range(t_new):
            cur = ext[POOL_HIST + t][:, sl]
            tot = cur
            for j in range(1, w):
                tot = tot + ext[POOL_HIST + t - j][:, sl]
            cnt = float(min(pos0 + t + 1, w))
            rows.append(tot / cnt - cur)
        pooled = jnp.concatenate(rows, axis=0).astype(BF16)
        y = _dot(pooled, w_ref[g]) * sc_ref[:, sl]
        nb = y.shape[0] // t_new
        for t in range(t_new):
            o_ref[t, :, sl] = y[t * nb:(t + 1) * nb]


def _pool_sample(hist_tm, u_tm, w_pool, scale, layer, pos0):
    t_new, nb, _ = u_tm.shape
    kern = functools.partial(_pool_sample_kernel, t_new=t_new, pos0=pos0)
    whole = lambda a: pl.BlockSpec(a.shape, lambda i: (0,) * a.ndim)
    return pl.pallas_call(
        kern,
        grid=(1,),
        in_specs=[whole(hist_tm), whole(u_tm), _layer_spec(w_pool, layer, (0, 0, 0)),
                  _layer_spec(scale, layer, (0, 0))],
        out_specs=pl.BlockSpec((t_new, nb, POOL_WIDTH), lambda i: (0, 0, 0)),
        out_shape=jax.ShapeDtypeStruct((t_new, nb, POOL_WIDTH), F32),
        name="pool_sample",
    )(hist_tm, u_tm, w_pool, scale)


def _attn_sample_kernel(pt_ref, lp_ref, gsub_ref, w_ref, kn_ref, vn_ref, ck_hbm, cv_hbm, o_ref,
                        kbuf, vbuf, sem, *, layer, lam_init, t_new, n_seq, n_groups):
    g_pages = PAGES_PER_STEP
    b = pl.program_id(0)
    w = w_ref[0]
    head_shift = (2 * t_new).bit_length() - 1
    tok_shift = N_HEADS.bit_length() - 1

    def own_head(shape):
        row = lax.broadcasted_iota(jnp.int32, shape, 0)
        col = lax.broadcasted_iota(jnp.int32, shape, 1)
        return (col & (N_HEADS - 1)) == (row >> head_shift), row, col

    def group_copies(seq, grp, slot):
        out = []
        for j in range(g_pages):
            page = pt_ref[seq, grp * g_pages + j]
            rows = pl.ds(j * PAGE_ROWS, PAGE_ROWS)
            out.append(pltpu.make_async_copy(ck_hbm.at[layer, page], kbuf.at[slot, rows], sem.at[slot, 0]))
            out.append(pltpu.make_async_copy(cv_hbm.at[layer, page], vbuf.at[slot, rows], sem.at[slot, 1]))
        return out

    @pl.when(b == 0)
    def _first():
        for c in group_copies(0, 0, 0):
            c.start()

    sn = _dot_nt(w, kn_ref[0])
    own, row, col = own_head(sn.shape)
    valid = own & ((col >> tok_shift) <= (row & (t_new - 1)))
    sn = jnp.where(valid, sn, NEG)
    m0 = jnp.max(sn, axis=1, keepdims=True)
    p0 = jnp.where(valid, jnp.exp2(sn - m0), 0.0)
    l0 = jnp.sum(p0, axis=1, keepdims=True)
    acc0 = _dot(p0.astype(BF16), vn_ref[0])

    def group(g, carry):
        m_old, l_old, acc_old = carry
        slot = g & 1
        last = g + 1 == n_groups
        nxt_seq = jnp.where(last, b + 1, b)
        nxt_grp = jnp.where(last, 0, g + 1)

        @pl.when(nxt_seq < n_seq)
        def _prefetch():
            for c in group_copies(nxt_seq, nxt_grp, 1 - slot):
                c.start()

        for c in group_copies(b, g, slot):
            c.wait()

        bias = jnp.where(own_head((w.shape[0], PAGE_ROWS))[0], 0.0, NEG)
        scores = []
        for j in range(g_pages):
            kj = kbuf[slot, j * PAGE_ROWS:(j + 1) * PAGE_ROWS, :].astype(BF16)
            scores.append(_dot_nt(w, kj) + bias)
        top = scores[0]
        for s in scores[1:]:
            top = jnp.maximum(top, s)
        m_new = jnp.maximum(m_old, jnp.max(top, axis=1, keepdims=True))
        alpha = jnp.exp2(m_old - m_new)
        tot = None
        acc = alpha * acc_old
        for j in range(g_pages):
            p = jnp.exp2(scores[j] - m_new)
            tot = p if tot is None else tot + p
            vj = vbuf[slot, j * PAGE_ROWS:(j + 1) * PAGE_ROWS, :].astype(BF16)
            acc = acc + _dot(p.astype(BF16), vj)
        l_new = alpha * l_old + jnp.sum(tot, axis=1, keepdims=True)
        return m_new, l_new, acc

    _, l, acc = lax.fori_loop(0, n_groups, group, (m0, l0, acc0))

    lam = _lam(lp_ref, lam_init)
    accn = acc / l
    per = 2 * t_new
    for hh in range(N_HEADS):
        blk = accn[hh * per:(hh + 1) * per]
        o = blk[0:t_new] - lam * blk[t_new:per]
        o_ref[0, :, hh * V_DIM:(hh + 1) * V_DIM] = _subln(o, gsub_ref[...], lam_init)


def _attn_sample(page_table, lp, gsub, w_blk, k_new, v_new, cache_k, cache_v, layer, lam_init,
                 t_new):
    nb, n_pages = page_table.shape
    g_pages = PAGES_PER_STEP
    n_groups = n_pages // g_pages
    n_rows = w_blk.shape[1]
    new_rows = k_new.shape[1]
    assert n_groups % 2 == 0
    kern = functools.partial(_attn_sample_kernel, layer=layer, lam_init=lam_init, t_new=t_new,
                             n_seq=nb, n_groups=n_groups)
    per_seq = lambda b, pt: (b, 0, 0)
    grid_spec = pltpu.PrefetchScalarGridSpec(
        num_scalar_prefetch=1,
        grid=(nb,),
        in_specs=[
            _layer_spec(lp, layer, (0, 0)),
            _layer_spec(gsub, layer, (0, 0)),
            pl.BlockSpec((1, n_rows, V_DIM), per_seq),
            pl.BlockSpec((1, new_rows, V_DIM), per_seq),
            pl.BlockSpec((1, new_rows, V_DIM), per_seq),
            pl.BlockSpec(memory_space=pl.ANY),
            pl.BlockSpec(memory_space=pl.ANY),
        ],
        out_specs=pl.BlockSpec((1, t_new, ATT_WIDTH), per_seq),
        scratch_shapes=[
            pltpu.VMEM((2, g_pages * PAGE_ROWS, V_DIM), F32),
            pltpu.VMEM((2, g_pages * PAGE_ROWS, V_DIM), F32),
            pltpu.SemaphoreType.DMA((2, 2)),
        ],
    )
    return pl.pallas_call(
        kern,
        grid_spec=grid_spec,
        out_shape=jax.ShapeDtypeStruct((nb, t_new, ATT_WIDTH), F32),
        compiler_params=pltpu.CompilerParams(
            dimension_semantics=("arbitrary",), vmem_limit_bytes=VMEM_LIMIT),
        name="attn_sample",
    )(page_table, lp, gsub, w_blk, k_new, v_new, cache_k, cache_v)


def _finish_kernel(x_ref, att_ref, pool_ref, p_ref, wo_ref, gpm_ref, gpf_ref, gqf_ref,
                   wg_ref, wu_ref, wd_ref, wpg_ref, wpp_ref, o_ref):
    mix = (_dot(att_ref[...].astype(BF16), wo_ref[0:ATT_WIDTH, :])
           + _dot(pool_ref[...].astype(BF16), wo_ref[ATT_WIDTH:, :]))
    x1 = x_ref[...] + _rms(mix, gpm_ref[...])
    h = _rms(x1, gpf_ref[...]).astype(BF16)
    f = None
    for sl in (slice(0, FF_SPLIT), slice(FF_SPLIT, D_FF)):
        a = (jax.nn.silu(_dot(h, wg_ref[:, sl])) * _dot(h, wu_ref[:, sl])).astype(BF16)
        part = _dot(a, wd_ref[sl, :])
        f = part if f is None else f + part
    x2 = x1 + _rms(f, gqf_ref[...])
    gate = jax.nn.sigmoid(_dot(x2.astype(BF16), wpg_ref[...]))
    o_ref[...] = x2 + gate * _dot(p_ref[...].astype(BF16), wpp_ref[...])


def _finish(x, att, pool, p, w_o, g_post_mix, g_pre_ffn, g_post_ffn, w_gate, w_up, w_down,
            w_ple_gate, w_ple_proj, layer, tm):
    m = x.shape[0]
    row = lambda i: (i, 0)
    resident = lambda a: _layer_spec(a, layer, (0, 0), pipeline_mode=pl.Buffered(1))
    return pl.pallas_call(
        _finish_kernel,
        grid=(m // tm,),
        in_specs=[
            pl.BlockSpec((tm, D_MODEL), row),
            pl.BlockSpec((tm, ATT_WIDTH), row),
            pl.BlockSpec((tm, POOL_WIDTH), row),
            pl.BlockSpec((None, tm, PLE_DIM), lambda i: (layer, i, 0)),
            resident(w_o), resident(g_post_mix), resident(g_pre_ffn), resident(g_post_ffn),
            resident(w_gate), resident(w_up), resident(w_down),
            resident(w_ple_gate), resident(w_ple_proj),
        ],
        out_specs=pl.BlockSpec((tm, D_MODEL), row),
        out_shape=jax.ShapeDtypeStruct((m, D_MODEL), F32),
        compiler_params=pltpu.CompilerParams(
            dimension_semantics=("arbitrary",), vmem_limit_bytes=VMEM_LIMIT),
        name="finish",
    )(x, att, pool, p, w_o, g_post_mix, g_pre_ffn, g_post_ffn, w_gate, w_up, w_down,
      w_ple_gate, w_ple_proj)


def _query_rows(q, nb, t_new):
    qr = q.reshape(nb, t_new, N_HEADS, 2, HEAD_DIM).transpose(0, 2, 3, 1, 4)
    eye = jnp.eye(2, dtype=q.dtype)
    w = qr[:, :, :, :, None, :] * eye[None, None, :, None, :, None]
    return w.reshape(nb, N_HEADS * 2 * t_new, V_DIM)


def kernel(x_prompt, x_sample, p_prompt, p_sample, cache_k, cache_v, state_pool, page_table, w_in, lambda_params, g_sub, w_pool, pool_scale, w_o, g_pre_mix, g_post_mix, g_pre_ffn, g_post_ffn, w_gate, w_up, w_down, w_ple_gate, w_ple_proj):
    batch, seq, _ = x_prompt.shape
    nb, t_new, _ = x_sample.shape
    depth = w_in.shape[0]
    n_pages = page_table.shape[1]
    past_len = n_pages * PAGE_SIZE
    assert seq % ROW_TILE == 0 and seq % Q_TILE == 0 and n_pages % PAGES_PER_STEP == 0
    assert ROW_TILE % KV_TILE == 0 and Q_TILE % KV_TILE == 0 and N_HEADS % HEADS_PER_STEP == 0
    assert t_new & (t_new - 1) == 0 and N_HEADS & (N_HEADS - 1) == 0

    tabs_p = _rope_tables(jnp.arange(seq))
    pos_s = past_len + jnp.arange(t_new)
    tabs_s = tuple(jnp.tile(a, (nb, 1)) for a in _rope_tables(pos_s))
    ck = cache_k.reshape(cache_k.shape[0], cache_k.shape[1], PAGE_ROWS, V_DIM)
    cv = cache_v.reshape(cache_v.shape[0], cache_v.shape[1], PAGE_ROWS, V_DIM)
    bf = lambda a: a.astype(BF16)
    rows = lambda a: a.reshape(depth, 1, -1)
    w_in_b, w_pool_b = bf(w_in), bf(w_pool)
    ffn_w = (bf(w_o), rows(g_post_mix), rows(g_pre_ffn), rows(g_post_ffn),
             bf(w_gate), bf(w_up), bf(w_down), bf(w_ple_gate), bf(w_ple_proj))
    g_in, gsub, scale = rows(g_pre_mix), rows(g_sub), rows(pool_scale)
    pp = p_prompt.reshape(depth, batch * seq, PLE_DIM)
    psm = p_sample.reshape(depth, nb * t_new, PLE_DIM)

    xp = x_prompt.reshape(batch * seq, D_MODEL)
    xs = x_sample.reshape(nb * t_new, D_MODEL)
    kp_rows, vp_rows, pp_rows, ks_rows, vs_rows, ps_rows = [], [], [], [], [], []
    for i in range(depth):
        lam_init = 0.8 - 0.6 * math.exp(-0.3 * i)

        q, kf, vf, kb, vt, pool, tail = _proj(xp, g_in, w_in_b, tabs_p, i, ROW_TILE,
                                              seq // ROW_TILE, (w_pool_b, scale))
        att = _attn_prompt(q, kb, vt, lambda_params, gsub, i, lam_init, batch, seq)
        xp = _finish(xp, att, pool, pp, *ffn_w, i, ROW_TILE)
        kp_rows.append(kf.reshape(batch, seq, N_HEADS, V_DIM))
        vp_rows.append(vf.reshape(batch, seq, N_HEADS, V_DIM))
        pp_rows.append(tail.reshape(batch, HIST_PAD, POOL_WIDTH)[:, HIST_PAD - POOL_HIST:])

        q, kf, vf, u = _proj(xs, g_in, w_in_b, tabs_s, i, nb * t_new, 1)
        new_rows = lambda a: bf(a.reshape(nb, t_new * N_HEADS, V_DIM))
        att = _attn_sample(page_table, lambda_params, gsub, _query_rows(q, nb, t_new),
                           new_rows(kf), new_rows(vf), ck, cv, i, lam_init, t_new)
        u3 = u.reshape(nb, t_new, POOL_WIDTH)
        pool = _pool_sample(state_pool[i].transpose(1, 0, 2), u3.transpose(1, 0, 2), w_pool_b,
                            scale, i, past_len)
        pool = pool.transpose(1, 0, 2).reshape(nb * t_new, POOL_WIDTH)
        xs = _finish(xs, att.reshape(nb * t_new, ATT_WIDTH), pool, psm, *ffn_w, i, nb * t_new)
        ks_rows.append(kf.reshape(nb, t_new, N_HEADS, V_DIM))
        vs_rows.append(vf.reshape(nb, t_new, N_HEADS, V_DIM))
        ps_rows.append(jnp.concatenate([state_pool[i], u3], axis=1)[:, t_new:])

    return (xp.reshape(batch, seq, D_MODEL), xs.reshape(nb, t_new, D_MODEL),
            jnp.stack(kp_rows), jnp.stack(vp_rows), jnp.stack(pp_rows),
            jnp.stack(ks_rows), jnp.stack(vs_rows), jnp.stack(ps_rows))
```

```python
import functools
import math

import jax
import jax.numpy as jnp
from jax import lax
from jax.experimental import pallas as pl
from jax.experimental.pallas import tpu as pltpu

D_MODEL = 1024
N_HEADS = 4
HEAD_DIM = 64
V_DIM = 2 * HEAD_DIM
ATT_WIDTH = N_HEADS * V_DIM
QK_WIDTH = N_HEADS * 2 * HEAD_DIM
ROT_DIM = HEAD_DIM // 4
ROT_HALF = ROT_DIM // 2
ROPE_THETA = 500000.0
ATT_SCALE = HEAD_DIM ** -0.5
Q_SCALE = ATT_SCALE * math.log2(math.e)
POOL_WINDOWS = (2, 4, 8, 16)
N_POOL_GROUPS = len(POOL_WINDOWS)
POOL_WIDTH = D_MODEL - ATT_WIDTH
POOL_GROUP = POOL_WIDTH // N_POOL_GROUPS
POOL_HIST = max(POOL_WINDOWS) - 1
HIST_PAD = POOL_HIST + 1
D_FF = 2816
MXU_DIM = 256
FF_CHUNK = 2 * MXU_DIM
PLE_DIM = 256
EPS = 1e-6
NEG = -1e30
PAGE_SIZE = 128
PAGE_ROWS = PAGE_SIZE * N_HEADS

LANES = 128
ROW_TILE = 512
Q_TILE = 512
KV_TILE = 512
ONES_ROWS = 16
VT_ROWS = V_DIM + ONES_ROWS
HEADS_PER_STEP = 4
PAGES_PER_GROUP = 8
VMEM_LIMIT = 56 * 1024 * 1024

F32 = jnp.float32
BF16 = jnp.bfloat16


def _dot(a, b):
    return jnp.dot(a, b, preferred_element_type=F32)


def _dot_nt(a, b):
    return lax.dot_general(a, b, (((1,), (1,)), ((), ())), preferred_element_type=F32)


def _rms(v, g):
    return v * lax.rsqrt(jnp.mean(v * v, axis=-1, keepdims=True) + EPS) * g


def _lam(lp_ref, lam_init):
    lp = lp_ref[...]
    s1 = jnp.sum(lp[0:1, :] * lp[1:2, :], axis=1, keepdims=True)
    s2 = jnp.sum(lp[2:3, :] * lp[3:4, :], axis=1, keepdims=True)
    return jnp.exp(s1) - jnp.exp(s2) + lam_init


def _subln(o, gsub, lam_init):
    y = o * lax.rsqrt(jnp.mean(o * o, axis=-1, keepdims=True) + EPS)
    return y * gsub * (1.0 - lam_init)


def _proj_kernel(*refs, tm, tiles_per_seq, for_prompt):
    if for_prompt:
        (x_ref, g_ref, w_ref, ta_ref, tb_ref, tc_ref, wp_ref, sc_ref,
         q_ref, kf_ref, vf_ref, kb_ref, vt_ref, pool_ref, tail_ref, ext_ref) = refs
    else:
        x_ref, g_ref, w_ref, ta_ref, tb_ref, tc_ref, q_ref, kf_ref, vf_ref, u_ref = refs
    h = _rms(x_ref[...], g_ref[...]).astype(BF16)
    ta, tb, tc = ta_ref[...], tb_ref[...], tc_ref[...]

    def rope(z):
        return z * ta + pltpu.roll(z, LANES - ROT_HALF, 1) * tb + pltpu.roll(z, ROT_HALF, 1) * tc

    def head_rows(hh):
        return pl.ds(hh, tm, stride=N_HEADS)

    zq = _dot(h, w_ref[:, 0:QK_WIDTH])
    for hh in range(N_HEADS):
        sl = slice(hh * V_DIM, (hh + 1) * V_DIM)
        q_ref[:, sl] = (rope(zq[:, sl]) * Q_SCALE).astype(BF16)
    zk = _dot(h, w_ref[:, QK_WIDTH:2 * QK_WIDTH])
    for hh in range(N_HEADS):
        sl = slice(hh * V_DIM, (hh + 1) * V_DIM)
        k = rope(zk[:, sl])
        kf_ref[head_rows(hh), :] = k
        if for_prompt:
            kb_ref[:, sl] = k.astype(BF16)
    zv = _dot(h, w_ref[:, 2 * QK_WIDTH:2 * QK_WIDTH + ATT_WIDTH])
    for hh in range(N_HEADS):
        vf_ref[head_rows(hh), :] = zv[:, hh * V_DIM:(hh + 1) * V_DIM]
    u = _dot(h, w_ref[:, 2 * QK_WIDTH + ATT_WIDTH:])
    if not for_prompt:
        u_ref[...] = u
    else:
        ones = jnp.ones((ONES_ROWS, KV_TILE), BF16)
        for c in range(tm // KV_TILE):
            vt = zv[c * KV_TILE:(c + 1) * KV_TILE, :].T.astype(BF16)
            for hh in range(N_HEADS):
                vt_ref[c, hh * VT_ROWS:hh * VT_ROWS + V_DIM, :] = vt[hh * V_DIM:(hh + 1) * V_DIM]
                vt_ref[c, hh * VT_ROWS + V_DIM:(hh + 1) * VT_ROWS, :] = ones
        _pool_rows(u, ext_ref, wp_ref, sc_ref, pool_ref, tail_ref, tm, tiles_per_seq)


def _pool_rows(u, ext_ref, w_ref, sc_ref, o_ref, tail_ref, tm, tiles_per_seq):
    it = pl.program_id(0) % tiles_per_seq

    @pl.when(it == 0)
    def _zero_history():
        ext_ref[0:HIST_PAD, :] = jnp.zeros((HIST_PAD, POOL_WIDTH), F32)

    @pl.when(it != 0)
    def _carry_history():
        ext_ref[0:HIST_PAD, :] = ext_ref[tm:tm + HIST_PAD, :]

    ext_ref[HIST_PAD:, :] = u
    tail_ref[...] = u[tm - HIST_PAD:, :]
    pos = it * tm + lax.broadcasted_iota(jnp.int32, (tm, POOL_GROUP), 0)
    for g, w in enumerate(POOL_WINDOWS):
        sl = slice(g * POOL_GROUP, (g + 1) * POOL_GROUP)
        cur = u[:, sl]
        tot = cur
        for j in range(1, w):
            tot = tot + ext_ref[HIST_PAD - j:HIST_PAD - j + tm, sl]
        cnt = jnp.minimum(pos + 1, w).astype(F32)
        pooled = (tot / cnt - cur).astype(BF16)
        o_ref[:, sl] = (_dot(pooled, w_ref[g]) * sc_ref[:, sl]).astype(o_ref.dtype)


def _layer_spec(a, layer, index_map_tail, **kw):
    return pl.BlockSpec((None,) + a.shape[1:], lambda *_: (layer,) + index_map_tail, **kw)


def _proj(x, g, w_in, tabs, layer, tm, tiles_per_seq, pool_w=None):
    m = x.shape[0]
    for_prompt = pool_w is not None
    row = lambda i: (i, 0)
    tab = lambda i: (i % tiles_per_seq, 0)
    wide = pl.BlockSpec((tm, ATT_WIDTH), row)
    head_major = pl.BlockSpec((tm * N_HEADS, V_DIM), row)
    rows_f32 = jax.ShapeDtypeStruct((m * N_HEADS, V_DIM), F32)
    in_specs = [
        pl.BlockSpec((tm, D_MODEL), row),
        _layer_spec(g, layer, (0, 0)),
        _layer_spec(w_in, layer, (0, 0)),
        pl.BlockSpec((tm, LANES), tab),
        pl.BlockSpec((tm, LANES), tab),
        pl.BlockSpec((tm, LANES), tab),
    ]
    args = [x, g, w_in, *tabs]
    out_specs = [wide, head_major, head_major]
    out_shape = [jax.ShapeDtypeStruct((m, QK_WIDTH), BF16), rows_f32, rows_f32]
    scratch = []
    if for_prompt:
        w_pool, scale = pool_w
        in_specs += [_layer_spec(w_pool, layer, (0, 0, 0)), _layer_spec(scale, layer, (0, 0))]
        args += [w_pool, scale]
        per = tm // KV_TILE
        out_specs += [wide,
                      pl.BlockSpec((per, N_HEADS * VT_ROWS, KV_TILE), lambda i: (i, 0, 0)),
                      wide,
                      pl.BlockSpec((HIST_PAD, POOL_WIDTH), lambda i: (i // tiles_per_seq, 0))]
        out_shape += [jax.ShapeDtypeStruct((m, QK_WIDTH), BF16),
                      jax.ShapeDtypeStruct((m // KV_TILE, N_HEADS * VT_ROWS, KV_TILE), BF16),
                      jax.ShapeDtypeStruct((m, POOL_WIDTH), BF16),
                      jax.ShapeDtypeStruct((m // tiles_per_seq // tm * HIST_PAD, POOL_WIDTH), F32)]
        scratch = [pltpu.VMEM((tm + HIST_PAD, POOL_WIDTH), F32)]
    else:
        out_specs += [wide]
        out_shape += [jax.ShapeDtypeStruct((m, POOL_WIDTH), F32)]
    return pl.pallas_call(
        functools.partial(_proj_kernel, tm=tm, tiles_per_seq=tiles_per_seq, for_prompt=for_prompt),
        grid=(m // tm,),
        in_specs=in_specs,
        out_specs=out_specs,
        out_shape=out_shape,
        scratch_shapes=scratch,
        compiler_params=pltpu.CompilerParams(
            dimension_semantics=("arbitrary",), vmem_limit_bytes=VMEM_LIMIT),
        name="proj",
    )(*args)


def _rope_tables(pos):
    lane = jnp.arange(LANES) % HEAD_DIM
    inv = jnp.power(ROPE_THETA, -(2 * (lane % ROT_HALF)).astype(F32) / ROT_DIM)
    ang = pos.astype(F32)[:, None] * inv[None, :]
    cos, sin = jnp.cos(ang), jnp.sin(ang)
    first = (lane < ROT_HALF)[None, :]
    second = ((lane >= ROT_HALF) & (lane < ROT_DIM))[None, :]
    ta = jnp.where(first | second, cos, 1.0)
    tb = jnp.where(first, -sin, 0.0)
    tc = jnp.where(second, sin, 0.0)
    return ta, tb, tc


def _attn_prompt_kernel(lp_ref, gsub_ref, q_ref, k_ref, vt_ref, o_ref,
                        qst_ref, m_ref, acc_ref, *, lam_init, tq, tk, nh):
    qi = pl.program_id(2)
    comp = lax.broadcasted_iota(jnp.int32, (V_DIM, tq), 0) < HEAD_DIM
    for hh in range(nh):
        qt = q_ref[:, hh * V_DIM:(hh + 1) * V_DIM].astype(F32).T
        qst_ref[hh, :, 0:tq] = jnp.where(comp, qt, 0.0).astype(BF16)
        qst_ref[hh, :, tq:] = jnp.where(comp, 0.0, qt).astype(BF16)
    m_ref[...] = jnp.full(m_ref.shape, NEG, F32)
    acc_ref[...] = jnp.zeros(acc_ref.shape, F32)

    def step(j, mask):
        start = pl.multiple_of(j * tk, tk)
        scores = []
        for hh in range(nh):
            s = _dot(k_ref[pl.ds(start, tk), hh * V_DIM:(hh + 1) * V_DIM], qst_ref[hh])
            scores.append(s if mask is None else jnp.where(mask, s, NEG))
        probs = []
        for hh in range(nh):
            m_old = m_ref[hh]
            m_new = jnp.maximum(m_old, jnp.max(scores[hh], axis=0, keepdims=True))
            m_ref[hh] = m_new
            probs.append((jnp.exp2(m_old - m_new), jnp.exp2(scores[hh] - m_new).astype(BF16)))
        for hh in range(nh):
            alpha, p = probs[hh]
            acc_ref[hh] = alpha * acc_ref[hh] + _dot(vt_ref[j, hh * VT_ROWS:(hh + 1) * VT_ROWS, :], p)

    per = tq // tk

    def body(j, carry):
        step(j, None)
        return carry

    lax.fori_loop(0, qi * per, body, 0)
    row = lax.broadcasted_iota(jnp.int32, (tk, 2 * tq), 0)
    col = lax.broadcasted_iota(jnp.int32, (tk, 2 * tq), 1)
    for d in range(per):
        step(qi * per + d, (d * tk + row) <= (col & (tq - 1)))

    lam = _lam(lp_ref, lam_init)
    for hh in range(nh):
        acc = acc_ref[hh, 0:V_DIM, :]
        l = acc_ref[hh, V_DIM:V_DIM + 1, :]
        ot = acc[:, 0:tq] / l[:, 0:tq] - lam * (acc[:, tq:] / l[:, tq:])
        o_ref[:, hh * V_DIM:(hh + 1) * V_DIM] = _subln(ot.T, gsub_ref[...], lam_init).astype(o_ref.dtype)


def _attn_prompt(q, k, vt, lp, gsub, layer, lam_init, batch, seq):
    tq, tk, nh = Q_TILE, KV_TILE, HEADS_PER_STEP
    nq = seq // tq
    kern = functools.partial(_attn_prompt_kernel, lam_init=lam_init, tq=tq, tk=tk, nh=nh)
    q_spec = pl.BlockSpec((tq, nh * V_DIM), lambda b, h, i: (b * nq + i, h))
    return pl.pallas_call(
        kern,
        grid=(batch, N_HEADS // nh, nq),
        in_specs=[
            _layer_spec(lp, layer, (0, 0)),
            _layer_spec(gsub, layer, (0, 0)),
            q_spec,
            pl.BlockSpec((seq, nh * V_DIM), lambda b, h, i: (b, h)),
            pl.BlockSpec((seq // tk, nh * VT_ROWS, tk), lambda b, h, i: (b, h, 0)),
        ],
        out_specs=q_spec,
        out_shape=jax.ShapeDtypeStruct((batch * seq, ATT_WIDTH), BF16),
        scratch_shapes=[
            pltpu.VMEM((nh, V_DIM, 2 * tq), BF16),
            pltpu.VMEM((nh, 1, 2 * tq), F32),
            pltpu.VMEM((nh, VT_ROWS, 2 * tq), F32),
        ],
        compiler_params=pltpu.CompilerParams(
            dimension_semantics=("arbitrary", "arbitrary", "arbitrary"),
            vmem_limit_bytes=VMEM_LIMIT),
        name="attn_prompt",
    )(lp, gsub, q, k, vt)


def _pool_sample_kernel(hist_ref, u_ref, w_ref, sc_ref, o_ref, *, t_new, pos0):
    ext = [hist_ref[j] for j in range(POOL_HIST)] + [u_ref[t] for t in range(t_new)]
    for g, w in enumerate(POOL_WINDOWS):
        sl = slice(g * POOL_GROUP, (g + 1) * POOL_GROUP)
        rows = []
        for t in range(t_new):
            cur = ext[POOL_HIST + t][:, sl]
            tot = cur
            for j in range(1, w):
                tot = tot + ext[POOL_HIST + t - j][:, sl]
            cnt = float(min(pos0 + t + 1, w))
            rows.append(tot / cnt - cur)
        pooled = jnp.concatenate(rows, axis=0).astype(BF16)
        y = _dot(pooled, w_ref[g]) * sc_ref[:, sl]
        nb = y.shape[0] // t_new
        for t in range(t_new):
            o_ref[t, :, sl] = y[t * nb:(t + 1) * nb]


def _pool_sample(hist_tm, u_tm, w_pool, scale, layer, pos0):
    t_new, nb, _ = u_tm.shape
    kern = functools.partial(_pool_sample_kernel, t_new=t_new, pos0=pos0)
    whole = lambda a: pl.BlockSpec(a.shape, lambda i: (0,) * a.ndim)
    return pl.pallas_call(
        kern,
        grid=(1,),
        in_specs=[whole(hist_tm), whole(u_tm), _layer_spec(w_pool, layer, (0, 0, 0)),
                  _layer_spec(scale, layer, (0, 0))],
        out_specs=pl.BlockSpec((t_new, nb, POOL_WIDTH), lambda i: (0, 0, 0)),
        out_shape=jax.ShapeDtypeStruct((t_new, nb, POOL_WIDTH), F32),
        name="pool_sample",
    )(hist_tm, u_tm, w_pool, scale)


def _sample_attention(pt_ref, lp_ref, gsub_ref, w_ref, kn_ref, vn_ref, ck_hbm, cv_hbm, o_ref,
                      kbuf, vbuf, sem, *, layer, lam_init, t_new, n_seq, n_groups):
    g_pages = PAGES_PER_GROUP
    b = pl.program_id(0)
    w = w_ref[0]
    head_shift = (2 * t_new).bit_length() - 1
    tok_shift = N_HEADS.bit_length() - 1

    def own_head(shape):
        row = lax.broadcasted_iota(jnp.int32, shape, 0)
        col = lax.broadcasted_iota(jnp.int32, shape, 1)
        return (col & (N_HEADS - 1)) == (row >> head_shift), row, col

    def group_copies(seq, grp, slot):
        out = []
        for j in range(g_pages):
            page = pt_ref[seq, grp * g_pages + j]
            rows = pl.ds(j * PAGE_ROWS, PAGE_ROWS)
            out.append(pltpu.make_async_copy(ck_hbm.at[layer, page], kbuf.at[slot, rows], sem.at[slot, 0]))
            out.append(pltpu.make_async_copy(cv_hbm.at[layer, page], vbuf.at[slot, rows], sem.at[slot, 1]))
        return out

    def first():
        @pl.when(b == 0)
        def _first():
            for c in group_copies(0, 0, 0):
                c.start()

    def init():
        sn = _dot_nt(w, kn_ref[0])
        own, row, col = own_head(sn.shape)
        valid = own & ((col >> tok_shift) <= (row & (t_new - 1)))
        sn = jnp.where(valid, sn, NEG)
        m0 = jnp.max(sn, axis=1, keepdims=True)
        p0 = jnp.where(valid, jnp.exp2(sn - m0), 0.0)
        return m0, jnp.sum(p0, axis=1, keepdims=True), _dot(p0.astype(BF16), vn_ref[0])

    def group(g, carry):
        m_old, l_old, acc_old = carry
        slot = g % 2
        if g + 1 < n_groups:
            for c in group_copies(b, g + 1, 1 - slot):
                c.start()
        else:
            @pl.when(b + 1 < n_seq)
            def _next_sequence():
                for c in group_copies(b + 1, 0, 1 - slot):
                    c.start()

        for c in group_copies(b, g, slot):
            c.wait()

        bias = jnp.where(own_head((w.shape[0], PAGE_ROWS))[0], 0.0, NEG)
        scores = []
        for j in range(g_pages):
            kj = kbuf[slot, j * PAGE_ROWS:(j + 1) * PAGE_ROWS, :].astype(BF16)
            scores.append(_dot_nt(w, kj) + bias)
        top = scores[0]
        for s in scores[1:]:
            top = jnp.maximum(top, s)
        m_new = jnp.maximum(m_old, jnp.max(top, axis=1, keepdims=True))
        alpha = jnp.exp2(m_old - m_new)
        tot = None
        acc = alpha * acc_old
        for j in range(g_pages):
            p = jnp.exp2(scores[j] - m_new)
            tot = p if tot is None else tot + p
            vj = vbuf[slot, j * PAGE_ROWS:(j + 1) * PAGE_ROWS, :].astype(BF16)
            acc = acc + _dot(p.astype(BF16), vj)
        l_new = alpha * l_old + jnp.sum(tot, axis=1, keepdims=True)
        return m_new, l_new, acc

    def finish(carry):
        _, l, acc = carry
        lam = _lam(lp_ref, lam_init)
        accn = acc / l
        per = 2 * t_new
        for hh in range(N_HEADS):
            blk = accn[hh * per:(hh + 1) * per]
            o = blk[0:t_new] - lam * blk[t_new:per]
            o_ref[0, :, hh * V_DIM:(hh + 1) * V_DIM] = _subln(o, gsub_ref[...], lam_init)

    return first, init, group, finish


N_FINISH_IN = 13


def _finish_kernel(*refs, sample):
    if sample is None:
        ins, o_ref = refs[:N_FINISH_IN], refs[N_FINISH_IN]
    else:
        ins = refs[1:1 + N_FINISH_IN]
        o_ref = refs[1 + N_FINISH_IN + 7]
    (x_ref, att_ref, pool_ref, p_ref, wo_ref, gpm_ref, gpf_ref, gqf_ref,
     wg_ref, wu_ref, wd_ref, wpg_ref, wpp_ref) = ins
    v = {}

    def mix():
        m = (_dot(att_ref[...].astype(BF16), wo_ref[0:ATT_WIDTH, :])
             + _dot(pool_ref[...].astype(BF16), wo_ref[ATT_WIDTH:, :]))
        v["x1"] = x_ref[...] + _rms(m, gpm_ref[...])
        v["h"] = _rms(v["x1"], gpf_ref[...]).astype(BF16)

    def ffn(sl):
        a = (jax.nn.silu(_dot(v["h"], wg_ref[:, sl])) * _dot(v["h"], wu_ref[:, sl])).astype(BF16)
        part = _dot(a, wd_ref[sl, :])
        v["f"] = part + v["f"] if "f" in v else part

    def ple():
        x2 = v["x1"] + _rms(v["f"], gqf_ref[...])
        gate = jax.nn.sigmoid(_dot(x2.astype(BF16), wpg_ref[...]))
        o_ref[...] = x2 + gate * _dot(p_ref[...].astype(BF16), wpp_ref[...])

    chunks = [slice(c, min(c + FF_CHUNK, D_FF)) for c in range(0, D_FF, FF_CHUNK)]
    stages = [mix] + [functools.partial(ffn, sl) for sl in chunks] + [ple]
    if sample is None:
        for stage in stages:
            stage()
        return

    pt_ref = refs[0]
    lp_ref, gsub_ref, wq_ref, kn_ref, vn_ref, ck_hbm, cv_hbm = refs[1 + N_FINISH_IN:1 + N_FINISH_IN + 7]
    os_ref, kbuf, vbuf, sem = refs[1 + N_FINISH_IN + 8:]
    n_groups = sample["n_groups"]
    first, init, group, finish = _sample_attention(
        pt_ref, lp_ref, gsub_ref, wq_ref, kn_ref, vn_ref, ck_hbm, cv_hbm, os_ref, kbuf, vbuf, sem,
        **sample)
    first()
    carry = init()
    done = 0
    for k, stage in enumerate(stages):
        upto = (k + 1) * n_groups // len(stages)
        for g in range(done, upto):
            carry = group(g, carry)
        done = upto
        stage()
    finish(carry)


def _finish(x, att, pool, p, w_o, g_post_mix, g_pre_ffn, g_post_ffn, w_gate, w_up, w_down,
            w_ple_gate, w_ple_proj, layer, tm, sample=None):
    m = x.shape[0]
    row = lambda i, *_: (i, 0)
    resident = lambda a: _layer_spec(a, layer, (0, 0), pipeline_mode=pl.Buffered(1))
    in_specs = [
        pl.BlockSpec((tm, D_MODEL), row),
        pl.BlockSpec((tm, ATT_WIDTH), row),
        pl.BlockSpec((tm, POOL_WIDTH), row),
        pl.BlockSpec((None, tm, PLE_DIM), lambda i, *_: (layer, i, 0)),
        resident(w_o), resident(g_post_mix), resident(g_pre_ffn), resident(g_post_ffn),
        resident(w_gate), resident(w_up), resident(w_down),
        resident(w_ple_gate), resident(w_ple_proj),
    ]
    args = [x, att, pool, p, w_o, g_post_mix, g_pre_ffn, g_post_ffn, w_gate, w_up, w_down,
            w_ple_gate, w_ple_proj]
    assert len(args) == N_FINISH_IN
    out_specs = [pl.BlockSpec((tm, D_MODEL), row)]
    out_shape = [jax.ShapeDtypeStruct((m, D_MODEL), F32)]
    scratch, prefetch, cfg = [], [], None
    if sample is not None:
        page_table, lp, gsub, w_blk, k_new, v_new, cache_k, cache_v, lam_init, t_new = sample
        nb, n_pages = page_table.shape
        n_groups = n_pages // PAGES_PER_GROUP
        assert nb == m // tm and n_pages % PAGES_PER_GROUP == 0 and n_groups % 2 == 0
        per_seq = lambda i, *_: (i, 0, 0)
        in_specs += [
            _layer_spec(lp, layer, (0, 0)),
            _layer_spec(gsub, layer, (0, 0)),
            pl.BlockSpec((1,) + w_blk.shape[1:], per_seq),
            pl.BlockSpec((1,) + k_new.shape[1:], per_seq),
            pl.BlockSpec((1,) + v_new.shape[1:], per_seq),
            pl.BlockSpec(memory_space=pl.ANY),
            pl.BlockSpec(memory_space=pl.ANY),
        ]
        args += [lp, gsub, w_blk, k_new, v_new, cache_k, cache_v]
        out_specs += [pl.BlockSpec((1, t_new, ATT_WIDTH), per_seq)]
        out_shape += [jax.ShapeDtypeStruct((nb, t_new, ATT_WIDTH), F32)]
        scratch = [
            pltpu.VMEM((2, PAGES_PER_GROUP * PAGE_ROWS, V_DIM), F32),
            pltpu.VMEM((2, PAGES_PER_GROUP * PAGE_ROWS, V_DIM), F32),
            pltpu.SemaphoreType.DMA((2, 2)),
        ]
        prefetch = [page_table]
        cfg = dict(layer=layer, lam_init=lam_init, t_new=t_new, n_seq=nb, n_groups=n_groups)
    grid_spec = pltpu.PrefetchScalarGridSpec(
        num_scalar_prefetch=len(prefetch), grid=(m // tm,), in_specs=in_specs,
        out_specs=out_specs, scratch_shapes=scratch)
    outs = pl.pallas_call(
        functools.partial(_finish_kernel, sample=cfg),
        grid_spec=grid_spec,
        out_shape=out_shape,
        compiler_params=pltpu.CompilerParams(
            dimension_semantics=("arbitrary",), vmem_limit_bytes=VMEM_LIMIT),
        name="finish",
    )(*prefetch, *args)
    return outs[0] if sample is None else outs


def _query_rows(q, nb, t_new):
    qr = q.reshape(nb, t_new, N_HEADS, 2, HEAD_DIM).transpose(0, 2, 3, 1, 4)
    eye = jnp.eye(2, dtype=q.dtype)
    w = qr[:, :, :, :, None, :] * eye[None, None, :, None, :, None]
    return w.reshape(nb, N_HEADS * 2 * t_new, V_DIM)


def kernel(x_prompt, x_sample, p_prompt, p_sample, cache_k, cache_v, state_pool, page_table, w_in, lambda_params, g_sub, w_pool, pool_scale, w_o, g_pre_mix, g_post_mix, g_pre_ffn, g_post_ffn, w_gate, w_up, w_down, w_ple_gate, w_ple_proj):
    batch, seq, _ = x_prompt.shape
    nb, t_new, _ = x_sample.shape
    depth = w_in.shape[0]
    n_pages = page_table.shape[1]
    past_len = n_pages * PAGE_SIZE
    assert seq % ROW_TILE == 0 and seq % Q_TILE == 0
    assert ROW_TILE % KV_TILE == 0 and Q_TILE % KV_TILE == 0 and N_HEADS % HEADS_PER_STEP == 0
    assert t_new & (t_new - 1) == 0 and N_HEADS & (N_HEADS - 1) == 0

    tabs_p = _rope_tables(jnp.arange(seq))
    pos_s = past_len + jnp.arange(t_new)
    tabs_s = tuple(jnp.tile(a, (nb, 1)) for a in _rope_tables(pos_s))
    ck = cache_k.reshape(cache_k.shape[0], cache_k.shape[1], PAGE_ROWS, V_DIM)
    cv = cache_v.reshape(cache_v.shape[0], cache_v.shape[1], PAGE_ROWS, V_DIM)
    bf = lambda a: a.astype(BF16)
    rows = lambda a: a.reshape(depth, 1, -1)
    w_in_b, w_pool_b = bf(w_in), bf(w_pool)
    ffn_w = (bf(w_o), rows(g_post_mix), rows(g_pre_ffn), rows(g_post_ffn),
             bf(w_gate), bf(w_up), bf(w_down), bf(w_ple_gate), bf(w_ple_proj))
    g_in, gsub, scale = rows(g_pre_mix), rows(g_sub), rows(pool_scale)
    pp = p_prompt.reshape(depth, batch * seq, PLE_DIM)
    psm = p_sample.reshape(depth, nb * t_new, PLE_DIM)

    xp = x_prompt.reshape(batch * seq, D_MODEL)
    xs = x_sample.reshape(nb * t_new, D_MODEL)
    kp_rows, vp_rows, pp_rows, ks_rows, vs_rows, ps_rows = [], [], [], [], [], []
    for i in range(depth):
        lam_init = 0.8 - 0.6 * math.exp(-0.3 * i)

        q, kf, vf, kb, vt, pool, tail = _proj(xp, g_in, w_in_b, tabs_p, i, ROW_TILE,
                                              seq // ROW_TILE, (w_pool_b, scale))
        att = _attn_prompt(q, kb, vt, lambda_params, gsub, i, lam_init, batch, seq)
        kp_rows.append(kf.reshape(batch, seq, N_HEADS, V_DIM))
        vp_rows.append(vf.reshape(batch, seq, N_HEADS, V_DIM))
        pp_rows.append(tail.reshape(batch, HIST_PAD, POOL_WIDTH)[:, HIST_PAD - POOL_HIST:])

        q, kf, vf, u = _proj(xs, g_in, w_in_b, tabs_s, i, nb * t_new, 1)
        new_rows = lambda a: bf(a.reshape(nb, t_new * N_HEADS, V_DIM))
        xp, att = _finish(xp, att, pool, pp, *ffn_w, i, ROW_TILE,
                          sample=(page_table, lambda_params, gsub, _query_rows(q, nb, t_new),
                                  new_rows(kf), new_rows(vf), ck, cv, lam_init, t_new))
        u3 = u.reshape(nb, t_new, POOL_WIDTH)
        pool = _pool_sample(state_pool[i].transpose(1, 0, 2), u3.transpose(1, 0, 2), w_pool_b,
                            scale, i, past_len)
        pool = pool.transpose(1, 0, 2).reshape(nb * t_new, POOL_WIDTH)
        xs = _finish(xs, att.reshape(nb * t_new, ATT_WIDTH), pool, psm, *ffn_w, i, nb * t_new)
        ks_rows.append(kf.reshape(nb, t_new, N_HEADS, V_DIM))
        vs_rows.append(vf.reshape(nb, t_new, N_HEADS, V_DIM))
        ps_rows.append(jnp.concatenate([state_pool[i], u3], axis=1)[:, t_new:])

    return (xp.reshape(batch, seq, D_MODEL), xs.reshape(nb, t_new, D_MODEL),
            jnp.stack(kp_rows), jnp.stack(vp_rows), jnp.stack(pp_rows),
            jnp.stack(ks_rows), jnp.stack(vs_rows), jnp.stack(ps_rows))
```

```python
import functools
import math

import jax
import jax.numpy as jnp
from jax import lax
from jax.experimental import pallas as pl
from jax.experimental.pallas import tpu as pltpu

D_MODEL = 1024
N_HEADS = 4
HEAD_DIM = 64
V_DIM = 2 * HEAD_DIM
ATT_WIDTH = N_HEADS * V_DIM
QK_WIDTH = N_HEADS * 2 * HEAD_DIM
ROT_DIM = HEAD_DIM // 4
ROT_HALF = ROT_DIM // 2
ROPE_THETA = 500000.0
ATT_SCALE = HEAD_DIM ** -0.5
Q_SCALE = ATT_SCALE * math.log2(math.e)
POOL_WINDOWS = (2, 4, 8, 16)
N_POOL_GROUPS = len(POOL_WINDOWS)
POOL_WIDTH = D_MODEL - ATT_WIDTH
POOL_GROUP = POOL_WIDTH // N_POOL_GROUPS
POOL_HIST = max(POOL_WINDOWS) - 1
HIST_PAD = POOL_HIST + 1
D_FF = 2816
MXU_DIM = 256
FF_CHUNK = MXU_DIM
PLE_DIM = 256
EPS = 1e-6
NEG = -1e30
PAGE_SIZE = 128
PAGE_ROWS = PAGE_SIZE * N_HEADS

LANES = 128
ROW_TILE = 512
Q_TILE = 512
KV_TILE = 512
ONES_ROWS = 16
VT_ROWS = V_DIM + ONES_ROWS
HEADS_PER_STEP = 4
PAGES_PER_GROUP = 8
VMEM_LIMIT = 56 * 1024 * 1024

F32 = jnp.float32
BF16 = jnp.bfloat16


def _dot(a, b):
    return jnp.dot(a, b, preferred_element_type=F32)


def _dot_nt(a, b):
    return lax.dot_general(a, b, (((1,), (1,)), ((), ())), preferred_element_type=F32)


def _rms(v, g):
    return v * lax.rsqrt(jnp.mean(v * v, axis=-1, keepdims=True) + EPS) * g


def _lam(lp_ref, lam_init):
    lp = lp_ref[...]
    s1 = jnp.sum(lp[0:1, :] * lp[1:2, :], axis=1, keepdims=True)
    s2 = jnp.sum(lp[2:3, :] * lp[3:4, :], axis=1, keepdims=True)
    return jnp.exp(s1) - jnp.exp(s2) + lam_init


def _subln(o, gsub, lam_init):
    y = o * lax.rsqrt(jnp.mean(o * o, axis=-1, keepdims=True) + EPS)
    return y * gsub * (1.0 - lam_init)


def _proj_kernel(*refs, tm, tiles_per_seq, for_prompt):
    if for_prompt:
        (x_ref, g_ref, w_ref, ta_ref, tb_ref, tc_ref, wp_ref, sc_ref,
         q_ref, kf_ref, vf_ref, kb_ref, vt_ref, pool_ref, tail_ref, ext_ref) = refs
    else:
        x_ref, g_ref, w_ref, ta_ref, tb_ref, tc_ref, q_ref, kf_ref, vf_ref, u_ref = refs
    h = _rms(x_ref[...], g_ref[...]).astype(BF16)
    ta, tb, tc = ta_ref[...], tb_ref[...], tc_ref[...]

    def rope(z):
        return z * ta + pltpu.roll(z, LANES - ROT_HALF, 1) * tb + pltpu.roll(z, ROT_HALF, 1) * tc

    def head_rows(hh):
        return pl.ds(hh, tm, stride=N_HEADS)

    zq = _dot(h, w_ref[:, 0:QK_WIDTH])
    for hh in range(N_HEADS):
        sl = slice(hh * V_DIM, (hh + 1) * V_DIM)
        q_ref[:, sl] = (rope(zq[:, sl]) * Q_SCALE).astype(BF16)
    zk = _dot(h, w_ref[:, QK_WIDTH:2 * QK_WIDTH])
    for hh in range(N_HEADS):
        sl = slice(hh * V_DIM, (hh + 1) * V_DIM)
        k = rope(zk[:, sl])
        kf_ref[head_rows(hh), :] = k
        if for_prompt:
            kb_ref[:, sl] = k.astype(BF16)
    zv = _dot(h, w_ref[:, 2 * QK_WIDTH:2 * QK_WIDTH + ATT_WIDTH])
    for hh in range(N_HEADS):
        vf_ref[head_rows(hh), :] = zv[:, hh * V_DIM:(hh + 1) * V_DIM]
    u = _dot(h, w_ref[:, 2 * QK_WIDTH + ATT_WIDTH:])
    if not for_prompt:
        u_ref[...] = u
    else:
        ones = jnp.ones((ONES_ROWS, KV_TILE), BF16)
        for c in range(tm // KV_TILE):
            vt = zv[c * KV_TILE:(c + 1) * KV_TILE, :].T.astype(BF16)
            for hh in range(N_HEADS):
                vt_ref[c, hh * VT_ROWS:hh * VT_ROWS + V_DIM, :] = vt[hh * V_DIM:(hh + 1) * V_DIM]
                vt_ref[c, hh * VT_ROWS + V_DIM:(hh + 1) * VT_ROWS, :] = ones
        _pool_rows(u, ext_ref, wp_ref, sc_ref, pool_ref, tail_ref, tm, tiles_per_seq)


def _pool_rows(u, ext_ref, w_ref, sc_ref, o_ref, tail_ref, tm, tiles_per_seq):
    it = pl.program_id(0) % tiles_per_seq

    @pl.when(it == 0)
    def _zero_history():
        ext_ref[0:HIST_PAD, :] = jnp.zeros((HIST_PAD, POOL_WIDTH), F32)

    @pl.when(it != 0)
    def _carry_history():
        ext_ref[0:HIST_PAD, :] = ext_ref[tm:tm + HIST_PAD, :]

    ext_ref[HIST_PAD:, :] = u
    tail_ref[...] = u[tm - HIST_PAD:, :]
    pos = it * tm + lax.broadcasted_iota(jnp.int32, (tm, POOL_GROUP), 0)
    for g, w in enumerate(POOL_WINDOWS):
        sl = slice(g * POOL_GROUP, (g + 1) * POOL_GROUP)
        cur = u[:, sl]
        tot = cur
        for j in range(1, w):
            tot = tot + ext_ref[HIST_PAD - j:HIST_PAD - j + tm, sl]
        cnt = jnp.minimum(pos + 1, w).astype(F32)
        pooled = (tot / cnt - cur).astype(BF16)
        o_ref[:, sl] = (_dot(pooled, w_ref[g]) * sc_ref[:, sl]).astype(o_ref.dtype)


def _layer_spec(a, layer, index_map_tail, **kw):
    return pl.BlockSpec((None,) + a.shape[1:], lambda *_: (layer,) + index_map_tail, **kw)


def _proj(x, g, w_in, tabs, layer, tm, tiles_per_seq, pool_w=None):
    m = x.shape[0]
    for_prompt = pool_w is not None
    row = lambda i: (i, 0)
    tab = lambda i: (i % tiles_per_seq, 0)
    wide = pl.BlockSpec((tm, ATT_WIDTH), row)
    head_major = pl.BlockSpec((tm * N_HEADS, V_DIM), row)
    rows_f32 = jax.ShapeDtypeStruct((m * N_HEADS, V_DIM), F32)
    in_specs = [
        pl.BlockSpec((tm, D_MODEL), row),
        _layer_spec(g, layer, (0, 0)),
        _layer_spec(w_in, layer, (0, 0)),
        pl.BlockSpec((tm, LANES), tab),
        pl.BlockSpec((tm, LANES), tab),
        pl.BlockSpec((tm, LANES), tab),
    ]
    args = [x, g, w_in, *tabs]
    out_specs = [wide, head_major, head_major]
    out_shape = [jax.ShapeDtypeStruct((m, QK_WIDTH), BF16), rows_f32, rows_f32]
    scratch = []
    if for_prompt:
        w_pool, scale = pool_w
        in_specs += [_layer_spec(w_pool, layer, (0, 0, 0)), _layer_spec(scale, layer, (0, 0))]
        args += [w_pool, scale]
        per = tm // KV_TILE
        out_specs += [wide,
                      pl.BlockSpec((per, N_HEADS * VT_ROWS, KV_TILE), lambda i: (i, 0, 0)),
                      wide,
                      pl.BlockSpec((HIST_PAD, POOL_WIDTH), lambda i: (i // tiles_per_seq, 0))]
        out_shape += [jax.ShapeDtypeStruct((m, QK_WIDTH), BF16),
                      jax.ShapeDtypeStruct((m // KV_TILE, N_HEADS * VT_ROWS, KV_TILE), BF16),
                      jax.ShapeDtypeStruct((m, POOL_WIDTH), BF16),
                      jax.ShapeDtypeStruct((m // tiles_per_seq // tm * HIST_PAD, POOL_WIDTH), F32)]
        scratch = [pltpu.VMEM((tm + HIST_PAD, POOL_WIDTH), F32)]
    else:
        out_specs += [wide]
        out_shape += [jax.ShapeDtypeStruct((m, POOL_WIDTH), F32)]
    return pl.pallas_call(
        functools.partial(_proj_kernel, tm=tm, tiles_per_seq=tiles_per_seq, for_prompt=for_prompt),
        grid=(m // tm,),
        in_specs=in_specs,
        out_specs=out_specs,
        out_shape=out_shape,
        scratch_shapes=scratch,
        compiler_params=pltpu.CompilerParams(
            dimension_semantics=("arbitrary",), vmem_limit_bytes=VMEM_LIMIT),
        name="proj",
    )(*args)


def _rope_tables(pos):
    lane = jnp.arange(LANES) % HEAD_DIM
    inv = jnp.power(ROPE_THETA, -(2 * (lane % ROT_HALF)).astype(F32) / ROT_DIM)
    ang = pos.astype(F32)[:, None] * inv[None, :]
    cos, sin = jnp.cos(ang), jnp.sin(ang)
    first = (lane < ROT_HALF)[None, :]
    second = ((lane >= ROT_HALF) & (lane < ROT_DIM))[None, :]
    ta = jnp.where(first | second, cos, 1.0)
    tb = jnp.where(first, -sin, 0.0)
    tc = jnp.where(second, sin, 0.0)
    return ta, tb, tc


def _attn_prompt_kernel(lp_ref, gsub_ref, q_ref, k_ref, vt_ref, o_ref,
                        qst_ref, m_ref, acc_ref, *, lam_init, tq, tk, nh):
    qi = pl.program_id(2)
    comp = lax.broadcasted_iota(jnp.int32, (V_DIM, tq), 0) < HEAD_DIM
    for hh in range(nh):
        qt = q_ref[:, hh * V_DIM:(hh + 1) * V_DIM].astype(F32).T
        qst_ref[hh, :, 0:tq] = jnp.where(comp, qt, 0.0).astype(BF16)
        qst_ref[hh, :, tq:] = jnp.where(comp, 0.0, qt).astype(BF16)
    m_ref[...] = jnp.full(m_ref.shape, NEG, F32)
    acc_ref[...] = jnp.zeros(acc_ref.shape, F32)

    def step(j, mask):
        start = pl.multiple_of(j * tk, tk)
        scores = []
        for hh in range(nh):
            s = _dot(k_ref[pl.ds(start, tk), hh * V_DIM:(hh + 1) * V_DIM], qst_ref[hh])
            scores.append(s if mask is None else jnp.where(mask, s, NEG))
        probs = []
        for hh in range(nh):
            m_old = m_ref[hh]
            m_new = jnp.maximum(m_old, jnp.max(scores[hh], axis=0, keepdims=True))
            m_ref[hh] = m_new
            probs.append((jnp.exp2(m_old - m_new), jnp.exp2(scores[hh] - m_new).astype(BF16)))
        for hh in range(nh):
            alpha, p = probs[hh]
            acc_ref[hh] = alpha * acc_ref[hh] + _dot(vt_ref[j, hh * VT_ROWS:(hh + 1) * VT_ROWS, :], p)

    per = tq // tk

    def body(j, carry):
        step(j, None)
        return carry

    lax.fori_loop(0, qi * per, body, 0)
    row = lax.broadcasted_iota(jnp.int32, (tk, 2 * tq), 0)
    col = lax.broadcasted_iota(jnp.int32, (tk, 2 * tq), 1)
    for d in range(per):
        step(qi * per + d, (d * tk + row) <= (col & (tq - 1)))

    lam = _lam(lp_ref, lam_init)
    for hh in range(nh):
        acc = acc_ref[hh, 0:V_DIM, :]
        l = acc_ref[hh, V_DIM:V_DIM + 1, :]
        ot = acc[:, 0:tq] / l[:, 0:tq] - lam * (acc[:, tq:] / l[:, tq:])
        o_ref[:, hh * V_DIM:(hh + 1) * V_DIM] = _subln(ot.T, gsub_ref[...], lam_init).astype(o_ref.dtype)


def _attn_prompt(q, k, vt, lp, gsub, layer, lam_init, batch, seq):
    tq, tk, nh = Q_TILE, KV_TILE, HEADS_PER_STEP
    nq = seq // tq
    kern = functools.partial(_attn_prompt_kernel, lam_init=lam_init, tq=tq, tk=tk, nh=nh)
    q_spec = pl.BlockSpec((tq, nh * V_DIM), lambda b, h, i: (b * nq + i, h))
    return pl.pallas_call(
        kern,
        grid=(batch, N_HEADS // nh, nq),
        in_specs=[
            _layer_spec(lp, layer, (0, 0)),
            _layer_spec(gsub, layer, (0, 0)),
            q_spec,
            pl.BlockSpec((seq, nh * V_DIM), lambda b, h, i: (b, h)),
            pl.BlockSpec((seq // tk, nh * VT_ROWS, tk), lambda b, h, i: (b, h, 0)),
        ],
        out_specs=q_spec,
        out_shape=jax.ShapeDtypeStruct((batch * seq, ATT_WIDTH), BF16),
        scratch_shapes=[
            pltpu.VMEM((nh, V_DIM, 2 * tq), BF16),
            pltpu.VMEM((nh, 1, 2 * tq), F32),
            pltpu.VMEM((nh, VT_ROWS, 2 * tq), F32),
        ],
        compiler_params=pltpu.CompilerParams(
            dimension_semantics=("arbitrary", "arbitrary", "arbitrary"),
            vmem_limit_bytes=VMEM_LIMIT),
        name="attn_prompt",
    )(lp, gsub, q, k, vt)


def _pool_sample_kernel(hist_ref, u_ref, w_ref, sc_ref, o_ref, *, t_new, pos0):
    ext = [hist_ref[j] for j in range(POOL_HIST)] + [u_ref[t] for t in range(t_new)]
    for g, w in enumerate(POOL_WINDOWS):
        sl = slice(g * POOL_GROUP, (g + 1) * POOL_GROUP)
        rows = []
        for t in range(t_new):
            cur = ext[POOL_HIST + t][:, sl]
            tot = cur
            for j in range(1, w):
                tot = tot + ext[POOL_HIST + t - j][:, sl]
            cnt = float(min(pos0 + t + 1, w))
            rows.append(tot / cnt - cur)
        pooled = jnp.concatenate(rows, axis=0).astype(BF16)
        y = _dot(pooled, w_ref[g]) * sc_ref[:, sl]
        nb = y.shape[0] // t_new
        for t in range(t_new):
            o_ref[t, :, sl] = y[t * nb:(t + 1) * nb]


def _pool_sample(hist_tm, u_tm, w_pool, scale, layer, pos0):
    t_new, nb, _ = u_tm.shape
    kern = functools.partial(_pool_sample_kernel, t_new=t_new, pos0=pos0)
    whole = lambda a: pl.BlockSpec(a.shape, lambda i: (0,) * a.ndim)
    return pl.pallas_call(
        kern,
        grid=(1,),
        in_specs=[whole(hist_tm), whole(u_tm), _layer_spec(w_pool, layer, (0, 0, 0)),
                  _layer_spec(scale, layer, (0, 0))],
        out_specs=pl.BlockSpec((t_new, nb, POOL_WIDTH), lambda i: (0, 0, 0)),
        out_shape=jax.ShapeDtypeStruct((t_new, nb, POOL_WIDTH), F32),
        name="pool_sample",
    )(hist_tm, u_tm, w_pool, scale)


def _sample_attention(pt_ref, lp_ref, gsub_ref, w_ref, kn_ref, vn_ref, ck_hbm, cv_hbm, o_ref,
                      kbuf, vbuf, sem, *, layer, lam_init, t_new, n_seq, n_groups):
    g_pages = PAGES_PER_GROUP
    b = pl.program_id(0)
    w = w_ref[0]
    head_shift = (2 * t_new).bit_length() - 1
    tok_shift = N_HEADS.bit_length() - 1

    def own_head(shape):
        row = lax.broadcasted_iota(jnp.int32, shape, 0)
        col = lax.broadcasted_iota(jnp.int32, shape, 1)
        return (col & (N_HEADS - 1)) == (row >> head_shift), row, col

    def group_copies(seq, grp, slot):
        out = []
        for j in range(g_pages):
            page = pt_ref[seq, grp * g_pages + j]
            rows = pl.ds(j * PAGE_ROWS, PAGE_ROWS)
            out.append(pltpu.make_async_copy(ck_hbm.at[layer, page], kbuf.at[slot, rows], sem.at[slot, 0]))
            out.append(pltpu.make_async_copy(cv_hbm.at[layer, page], vbuf.at[slot, rows], sem.at[slot, 1]))
        return out

    def first():
        @pl.when(b == 0)
        def _first():
            for c in group_copies(0, 0, 0):
                c.start()

    def init():
        sn = _dot_nt(w, kn_ref[0])
        own, row, col = own_head(sn.shape)
        valid = own & ((col >> tok_shift) <= (row & (t_new - 1)))
        sn = jnp.where(valid, sn, NEG)
        m0 = jnp.max(sn, axis=1, keepdims=True)
        p0 = jnp.where(valid, jnp.exp2(sn - m0), 0.0)
        return m0, jnp.sum(p0, axis=1, keepdims=True), _dot(p0.astype(BF16), vn_ref[0])

    def group(g, carry):
        m_old, l_old, acc_old = carry
        slot = g % 2
        if g + 1 < n_groups:
            for c in group_copies(b, g + 1, 1 - slot):
                c.start()
        else:
            @pl.when(b + 1 < n_seq)
            def _next_sequence():
                for c in group_copies(b + 1, 0, 1 - slot):
                    c.start()

        for c in group_copies(b, g, slot):
            c.wait()

        bias = jnp.where(own_head((w.shape[0], PAGE_ROWS))[0], 0.0, NEG)
        scores = []
        for j in range(g_pages):
            kj = kbuf[slot, j * PAGE_ROWS:(j + 1) * PAGE_ROWS, :].astype(BF16)
            scores.append(_dot_nt(w, kj) + bias)
        top = scores[0]
        for s in scores[1:]:
            top = jnp.maximum(top, s)
        m_new = jnp.maximum(m_old, jnp.max(top, axis=1, keepdims=True))
        alpha = jnp.exp2(m_old - m_new)
        tot = None
        acc = alpha * acc_old
        for j in range(g_pages):
            p = jnp.exp2(scores[j] - m_new)
            tot = p if tot is None else tot + p
            vj = vbuf[slot, j * PAGE_ROWS:(j + 1) * PAGE_ROWS, :].astype(BF16)
            acc = acc + _dot(p.astype(BF16), vj)
        l_new = alpha * l_old + jnp.sum(tot, axis=1, keepdims=True)
        return m_new, l_new, acc

    def finish(carry):
        _, l, acc = carry
        lam = _lam(lp_ref, lam_init)
        accn = acc / l
        per = 2 * t_new
        for hh in range(N_HEADS):
            blk = accn[hh * per:(hh + 1) * per]
            o = blk[0:t_new] - lam * blk[t_new:per]
            o_ref[0, :, hh * V_DIM:(hh + 1) * V_DIM] = _subln(o, gsub_ref[...], lam_init)

    return first, init, group, finish


N_FINISH_IN = 13


def _finish_kernel(*refs, sample):
    if sample is None:
        ins, o_ref = refs[:N_FINISH_IN], refs[N_FINISH_IN]
    else:
        ins = refs[1:1 + N_FINISH_IN]
        o_ref = refs[1 + N_FINISH_IN + 7]
    (x_ref, att_ref, pool_ref, p_ref, wo_ref, gpm_ref, gpf_ref, gqf_ref,
     wg_ref, wu_ref, wd_ref, wpg_ref, wpp_ref) = ins
    v = {}

    def mix_att():
        v["mix"] = _dot(att_ref[...].astype(BF16), wo_ref[0:ATT_WIDTH, :])

    def mix_pool():
        m = v["mix"] + _dot(pool_ref[...].astype(BF16), wo_ref[ATT_WIDTH:, :])
        v["x1"] = x_ref[...] + _rms(m, gpm_ref[...])
        v["h"] = _rms(v["x1"], gpf_ref[...]).astype(BF16)

    def ffn(sl):
        a = (jax.nn.silu(_dot(v["h"], wg_ref[:, sl])) * _dot(v["h"], wu_ref[:, sl])).astype(BF16)
        part = _dot(a, wd_ref[sl, :])
        v["f"] = part + v["f"] if "f" in v else part

    def ple(sl):
        if "x2" not in v:
            v["x2"] = v["x1"] + _rms(v["f"], gqf_ref[...])
            v["x2b"] = v["x2"].astype(BF16)
            v["pb"] = p_ref[...].astype(BF16)
        gate = jax.nn.sigmoid(_dot(v["x2b"], wpg_ref[:, sl]))
        o_ref[:, sl] = v["x2"][:, sl] + gate * _dot(v["pb"], wpp_ref[:, sl])

    chunks = [slice(c, min(c + FF_CHUNK, D_FF)) for c in range(0, D_FF, FF_CHUNK)]
    halves = [slice(0, D_MODEL // 2), slice(D_MODEL // 2, D_MODEL)]
    stages = ([mix_att, mix_pool] + [functools.partial(ffn, sl) for sl in chunks]
              + [functools.partial(ple, sl) for sl in halves])
    if sample is None:
        for stage in stages:
            stage()
        return

    pt_ref = refs[0]
    lp_ref, gsub_ref, wq_ref, kn_ref, vn_ref, ck_hbm, cv_hbm = refs[1 + N_FINISH_IN:1 + N_FINISH_IN + 7]
    os_ref, kbuf, vbuf, sem = refs[1 + N_FINISH_IN + 8:]
    n_groups = sample["n_groups"]
    first, init, group, finish = _sample_attention(
        pt_ref, lp_ref, gsub_ref, wq_ref, kn_ref, vn_ref, ck_hbm, cv_hbm, os_ref, kbuf, vbuf, sem,
        **sample)
    first()
    carry = init()
    done = 0
    for k, stage in enumerate(stages):
        upto = (k + 1) * n_groups // len(stages)
        for g in range(done, upto):
            carry = group(g, carry)
        done = upto
        stage()
    finish(carry)


def _finish(x, att, pool, p, w_o, g_post_mix, g_pre_ffn, g_post_ffn, w_gate, w_up, w_down,
            w_ple_gate, w_ple_proj, layer, tm, sample=None):
    m = x.shape[0]
    row = lambda i, *_: (i, 0)
    resident = lambda a: _layer_spec(a, layer, (0, 0), pipeline_mode=pl.Buffered(1))
    in_specs = [
        pl.BlockSpec((tm, D_MODEL), row),
        pl.BlockSpec((tm, ATT_WIDTH), row),
        pl.BlockSpec((tm, POOL_WIDTH), row),
        pl.BlockSpec((None, tm, PLE_DIM), lambda i, *_: (layer, i, 0)),
        resident(w_o), resident(g_post_mix), resident(g_pre_ffn), resident(g_post_ffn),
        resident(w_gate), resident(w_up), resident(w_down),
        resident(w_ple_gate), resident(w_ple_proj),
    ]
    args = [x, att, pool, p, w_o, g_post_mix, g_pre_ffn, g_post_ffn, w_gate, w_up, w_down,
            w_ple_gate, w_ple_proj]
    assert len(args) == N_FINISH_IN
    out_specs = [pl.BlockSpec((tm, D_MODEL), row)]
    out_shape = [jax.ShapeDtypeStruct((m, D_MODEL), F32)]
    scratch, prefetch, cfg = [], [], None
    if sample is not None:
        page_table, lp, gsub, w_blk, k_new, v_new, cache_k, cache_v, lam_init, t_new = sample
        nb, n_pages = page_table.shape
        n_groups = n_pages // PAGES_PER_GROUP
        assert nb == m // tm and n_pages % PAGES_PER_GROUP == 0 and n_groups % 2 == 0
        per_seq = lambda i, *_: (i, 0, 0)
        in_specs += [
            _layer_spec(lp, layer, (0, 0)),
            _layer_spec(gsub, layer, (0, 0)),
            pl.BlockSpec((1,) + w_blk.shape[1:], per_seq),
            pl.BlockSpec((1,) + k_new.shape[1:], per_seq),
            pl.BlockSpec((1,) + v_new.shape[1:], per_seq),
            pl.BlockSpec(memory_space=pl.ANY),
            pl.BlockSpec(memory_space=pl.ANY),
        ]
        args += [lp, gsub, w_blk, k_new, v_new, cache_k, cache_v]
        out_specs += [pl.BlockSpec((1, t_new, ATT_WIDTH), per_seq)]
        out_shape += [jax.ShapeDtypeStruct((nb, t_new, ATT_WIDTH), F32)]
        scratch = [
            pltpu.VMEM((2, PAGES_PER_GROUP * PAGE_ROWS, V_DIM), F32),
            pltpu.VMEM((2, PAGES_PER_GROUP * PAGE_ROWS, V_DIM), F32),
            pltpu.SemaphoreType.DMA((2, 2)),
        ]
        prefetch = [page_table]
        cfg = dict(layer=layer, lam_init=lam_init, t_new=t_new, n_seq=nb, n_groups=n_groups)
    grid_spec = pltpu.PrefetchScalarGridSpec(
        num_scalar_prefetch=len(prefetch), grid=(m // tm,), in_specs=in_specs,
        out_specs=out_specs, scratch_shapes=scratch)
    outs = pl.pallas_call(
        functools.partial(_finish_kernel, sample=cfg),
        grid_spec=grid_spec,
        out_shape=out_shape,
        compiler_params=pltpu.CompilerParams(
            dimension_semantics=("arbitrary",), vmem_limit_bytes=VMEM_LIMIT),
        name="finish",
    )(*prefetch, *args)
    return outs[0] if sample is None else outs


def _query_rows(q, nb, t_new):
    qr = q.reshape(nb, t_new, N_HEADS, 2, HEAD_DIM).transpose(0, 2, 3, 1, 4)
    eye = jnp.eye(2, dtype=q.dtype)
    w = qr[:, :, :, :, None, :] * eye[None, None, :, None, :, None]
    return w.reshape(nb, N_HEADS * 2 * t_new, V_DIM)


def kernel(x_prompt, x_sample, p_prompt, p_sample, cache_k, cache_v, state_pool, page_table, w_in, lambda_params, g_sub, w_pool, pool_scale, w_o, g_pre_mix, g_post_mix, g_pre_ffn, g_post_ffn, w_gate, w_up, w_down, w_ple_gate, w_ple_proj):
    batch, seq, _ = x_prompt.shape
    nb, t_new, _ = x_sample.shape
    depth = w_in.shape[0]
    n_pages = page_table.shape[1]
    past_len = n_pages * PAGE_SIZE
    assert seq % ROW_TILE == 0 and seq % Q_TILE == 0
    assert ROW_TILE % KV_TILE == 0 and Q_TILE % KV_TILE == 0 and N_HEADS % HEADS_PER_STEP == 0
    assert t_new & (t_new - 1) == 0 and N_HEADS & (N_HEADS - 1) == 0

    tabs_p = _rope_tables(jnp.arange(seq))
    pos_s = past_len + jnp.arange(t_new)
    tabs_s = tuple(jnp.tile(a, (nb, 1)) for a in _rope_tables(pos_s))
    ck = cache_k.reshape(cache_k.shape[0], cache_k.shape[1], PAGE_ROWS, V_DIM)
    cv = cache_v.reshape(cache_v.shape[0], cache_v.shape[1], PAGE_ROWS, V_DIM)
    bf = lambda a: a.astype(BF16)
    rows = lambda a: a.reshape(depth, 1, -1)
    w_in_b, w_pool_b = bf(w_in), bf(w_pool)
    ffn_w = (bf(w_o), rows(g_post_mix), rows(g_pre_ffn), rows(g_post_ffn),
             bf(w_gate), bf(w_up), bf(w_down), bf(w_ple_gate), bf(w_ple_proj))
    g_in, gsub, scale = rows(g_pre_mix), rows(g_sub), rows(pool_scale)
    pp = p_prompt.reshape(depth, batch * seq, PLE_DIM)
    psm = p_sample.reshape(depth, nb * t_new, PLE_DIM)

    xp = x_prompt.reshape(batch * seq, D_MODEL)
    xs = x_sample.reshape(nb * t_new, D_MODEL)
    kp_rows, vp_rows, pp_rows, ks_rows, vs_rows, ps_rows = [], [], [], [], [], []
    for i in range(depth):
        lam_init = 0.8 - 0.6 * math.exp(-0.3 * i)

        q, kf, vf, kb, vt, pool, tail = _proj(xp, g_in, w_in_b, tabs_p, i, ROW_TILE,
                                              seq // ROW_TILE, (w_pool_b, scale))
        att = _attn_prompt(q, kb, vt, lambda_params, gsub, i, lam_init, batch, seq)
        kp_rows.append(kf.reshape(batch, seq, N_HEADS, V_DIM))
        vp_rows.append(vf.reshape(batch, seq, N_HEADS, V_DIM))
        pp_rows.append(tail.reshape(batch, HIST_PAD, POOL_WIDTH)[:, HIST_PAD - POOL_HIST:])

        q, kf, vf, u = _proj(xs, g_in, w_in_b, tabs_s, i, nb * t_new, 1)
        new_rows = lambda a: bf(a.reshape(nb, t_new * N_HEADS, V_DIM))
        xp, att = _finish(xp, att, pool, pp, *ffn_w, i, ROW_TILE,
                          sample=(page_table, lambda_params, gsub, _query_rows(q, nb, t_new),
                                  new_rows(kf), new_rows(vf), ck, cv, lam_init, t_new))
        u3 = u.reshape(nb, t_new, POOL_WIDTH)
        pool = _pool_sample(state_pool[i].transpose(1, 0, 2), u3.transpose(1, 0, 2), w_pool_b,
                            scale, i, past_len)
        pool = pool.transpose(1, 0, 2).reshape(nb * t_new, POOL_WIDTH)
        xs = _finish(xs, att.reshape(nb * t_new, ATT_WIDTH), pool, psm, *ffn_w, i, nb * t_new)
        ks_rows.append(kf.reshape(nb, t_new, N_HEADS, V_DIM))
        vs_rows.append(vf.reshape(nb, t_new, N_HEADS, V_DIM))
        ps_rows.append(jnp.concatenate([state_pool[i], u3], axis=1)[:, t_new:])

    return (xp.reshape(batch, seq, D_MODEL), xs.reshape(nb, t_new, D_MODEL),
            jnp.stack(kp_rows), jnp.stack(vp_rows), jnp.stack(pp_rows),
            jnp.stack(ks_rows), jnp.stack(vs_rows), jnp.stack(ps_rows))
```

```python
import functools
import math

import jax
import jax.numpy as jnp
from jax import lax
from jax.experimental import pallas as pl
from jax.experimental.pallas import tpu as pltpu

D_MODEL = 1024
N_HEADS = 4
HEAD_DIM = 64
V_DIM = 2 * HEAD_DIM
ATT_WIDTH = N_HEADS * V_DIM
QK_WIDTH = N_HEADS * 2 * HEAD_DIM
ROT_DIM = HEAD_DIM // 4
ROT_HALF = ROT_DIM // 2
ROPE_THETA = 500000.0
ATT_SCALE = HEAD_DIM ** -0.5
Q_SCALE = ATT_SCALE * math.log2(math.e)
POOL_WINDOWS = (2, 4, 8, 16)
N_POOL_GROUPS = len(POOL_WINDOWS)
POOL_WIDTH = D_MODEL - ATT_WIDTH
POOL_GROUP = POOL_WIDTH // N_POOL_GROUPS
POOL_HIST = max(POOL_WINDOWS) - 1
HIST_PAD = POOL_HIST + 1
D_FF = 2816
MXU_DIM = 256
FF_CHUNK = MXU_DIM
PLE_DIM = 256
EPS = 1e-6
NEG = -1e30
PAGE_SIZE = 128
PAGE_ROWS = PAGE_SIZE * N_HEADS

LANES = 128
ROW_TILE = 512
Q_TILE = 512
KV_TILE = 512
ONES_ROWS = 16
VT_ROWS = V_DIM + ONES_ROWS
HEADS_PER_STEP = 4
PAGES_PER_GROUP = 8
GROUP_SLOTS = 4
VMEM_LIMIT = 56 * 1024 * 1024

F32 = jnp.float32
BF16 = jnp.bfloat16


def _dot(a, b):
    return jnp.dot(a, b, preferred_element_type=F32)


def _dot_nt(a, b):
    return lax.dot_general(a, b, (((1,), (1,)), ((), ())), preferred_element_type=F32)


def _rms(v, g):
    return v * lax.rsqrt(jnp.mean(v * v, axis=-1, keepdims=True) + EPS) * g


def _lam(lp_ref, lam_init):
    lp = lp_ref[...]
    s1 = jnp.sum(lp[0:1, :] * lp[1:2, :], axis=1, keepdims=True)
    s2 = jnp.sum(lp[2:3, :] * lp[3:4, :], axis=1, keepdims=True)
    return jnp.exp(s1) - jnp.exp(s2) + lam_init


def _subln(o, gsub, lam_init):
    y = o * lax.rsqrt(jnp.mean(o * o, axis=-1, keepdims=True) + EPS)
    return y * gsub * (1.0 - lam_init)


def _proj_kernel(*refs, tm, tiles_per_seq, for_prompt):
    if for_prompt:
        (x_ref, g_ref, w_ref, ta_ref, tb_ref, tc_ref, wp_ref, sc_ref,
         q_ref, kf_ref, vf_ref, kb_ref, vt_ref, pool_ref, tail_ref, ext_ref) = refs
    else:
        x_ref, g_ref, w_ref, ta_ref, tb_ref, tc_ref, q_ref, kf_ref, vf_ref, u_ref = refs
    h = _rms(x_ref[...], g_ref[...]).astype(BF16)
    ta, tb, tc = ta_ref[...], tb_ref[...], tc_ref[...]

    def rope(z):
        return z * ta + pltpu.roll(z, LANES - ROT_HALF, 1) * tb + pltpu.roll(z, ROT_HALF, 1) * tc

    def head_rows(hh):
        return pl.ds(hh, tm, stride=N_HEADS)

    zq = _dot(h, w_ref[:, 0:QK_WIDTH])
    for hh in range(N_HEADS):
        sl = slice(hh * V_DIM, (hh + 1) * V_DIM)
        q_ref[:, sl] = (rope(zq[:, sl]) * Q_SCALE).astype(BF16)
    zk = _dot(h, w_ref[:, QK_WIDTH:2 * QK_WIDTH])
    for hh in range(N_HEADS):
        sl = slice(hh * V_DIM, (hh + 1) * V_DIM)
        k = rope(zk[:, sl])
        kf_ref[head_rows(hh), :] = k
        if for_prompt:
            kb_ref[:, sl] = k.astype(BF16)
    zv = _dot(h, w_ref[:, 2 * QK_WIDTH:2 * QK_WIDTH + ATT_WIDTH])
    for hh in range(N_HEADS):
        vf_ref[head_rows(hh), :] = zv[:, hh * V_DIM:(hh + 1) * V_DIM]
    u = _dot(h, w_ref[:, 2 * QK_WIDTH + ATT_WIDTH:])
    if not for_prompt:
        u_ref[...] = u
    else:
        ones = jnp.ones((ONES_ROWS, KV_TILE), BF16)
        for c in range(tm // KV_TILE):
            vt = zv[c * KV_TILE:(c + 1) * KV_TILE, :].T.astype(BF16)
            for hh in range(N_HEADS):
                vt_ref[c, hh * VT_ROWS:hh * VT_ROWS + V_DIM, :] = vt[hh * V_DIM:(hh + 1) * V_DIM]
                vt_ref[c, hh * VT_ROWS + V_DIM:(hh + 1) * VT_ROWS, :] = ones
        _pool_rows(u, ext_ref, wp_ref, sc_ref, pool_ref, tail_ref, tm, tiles_per_seq)


def _pool_rows(u, ext_ref, w_ref, sc_ref, o_ref, tail_ref, tm, tiles_per_seq):
    it = pl.program_id(0) % tiles_per_seq

    @pl.when(it == 0)
    def _zero_history():
        ext_ref[0:HIST_PAD, :] = jnp.zeros((HIST_PAD, POOL_WIDTH), F32)

    @pl.when(it != 0)
    def _carry_history():
        ext_ref[0:HIST_PAD, :] = ext_ref[tm:tm + HIST_PAD, :]

    ext_ref[HIST_PAD:, :] = u
    tail_ref[...] = u[tm - HIST_PAD:, :]
    pos = it * tm + lax.broadcasted_iota(jnp.int32, (tm, POOL_GROUP), 0)
    for g, w in enumerate(POOL_WINDOWS):
        sl = slice(g * POOL_GROUP, (g + 1) * POOL_GROUP)
        cur = u[:, sl]
        tot = cur
        for j in range(1, w):
            tot = tot + ext_ref[HIST_PAD - j:HIST_PAD - j + tm, sl]
        cnt = jnp.minimum(pos + 1, w).astype(F32)
        pooled = (tot / cnt - cur).astype(BF16)
        o_ref[:, sl] = (_dot(pooled, w_ref[g]) * sc_ref[:, sl]).astype(o_ref.dtype)


def _layer_spec(a, layer, index_map_tail, **kw):
    return pl.BlockSpec((None,) + a.shape[1:], lambda *_: (layer,) + index_map_tail, **kw)


def _proj(x, g, w_in, tabs, layer, tm, tiles_per_seq, pool_w=None):
    m = x.shape[0]
    for_prompt = pool_w is not None
    row = lambda i: (i, 0)
    tab = lambda i: (i % tiles_per_seq, 0)
    wide = pl.BlockSpec((tm, ATT_WIDTH), row)
    head_major = pl.BlockSpec((tm * N_HEADS, V_DIM), row)
    rows_f32 = jax.ShapeDtypeStruct((m * N_HEADS, V_DIM), F32)
    in_specs = [
        pl.BlockSpec((tm, D_MODEL), row),
        _layer_spec(g, layer, (0, 0)),
        _layer_spec(w_in, layer, (0, 0)),
        pl.BlockSpec((tm, LANES), tab),
        pl.BlockSpec((tm, LANES), tab),
        pl.BlockSpec((tm, LANES), tab),
    ]
    args = [x, g, w_in, *tabs]
    out_specs = [wide, head_major, head_major]
    out_shape = [jax.ShapeDtypeStruct((m, QK_WIDTH), BF16), rows_f32, rows_f32]
    scratch = []
    if for_prompt:
        w_pool, scale = pool_w
        in_specs += [_layer_spec(w_pool, layer, (0, 0, 0)), _layer_spec(scale, layer, (0, 0))]
        args += [w_pool, scale]
        per = tm // KV_TILE
        out_specs += [wide,
                      pl.BlockSpec((per, N_HEADS * VT_ROWS, KV_TILE), lambda i: (i, 0, 0)),
                      wide,
                      pl.BlockSpec((HIST_PAD, POOL_WIDTH), lambda i: (i // tiles_per_seq, 0))]
        out_shape += [jax.ShapeDtypeStruct((m, QK_WIDTH), BF16),
                      jax.ShapeDtypeStruct((m // KV_TILE, N_HEADS * VT_ROWS, KV_TILE), BF16),
                      jax.ShapeDtypeStruct((m, POOL_WIDTH), BF16),
                      jax.ShapeDtypeStruct((m // tiles_per_seq // tm * HIST_PAD, POOL_WIDTH), F32)]
        scratch = [pltpu.VMEM((tm + HIST_PAD, POOL_WIDTH), F32)]
    else:
        out_specs += [wide]
        out_shape += [jax.ShapeDtypeStruct((m, POOL_WIDTH), F32)]
    return pl.pallas_call(
        functools.partial(_proj_kernel, tm=tm, tiles_per_seq=tiles_per_seq, for_prompt=for_prompt),
        grid=(m // tm,),
        in_specs=in_specs,
        out_specs=out_specs,
        out_shape=out_shape,
        scratch_shapes=scratch,
        compiler_params=pltpu.CompilerParams(
            dimension_semantics=("arbitrary",), vmem_limit_bytes=VMEM_LIMIT),
        name="proj",
    )(*args)


def _rope_tables(pos):
    lane = jnp.arange(LANES) % HEAD_DIM
    inv = jnp.power(ROPE_THETA, -(2 * (lane % ROT_HALF)).astype(F32) / ROT_DIM)
    ang = pos.astype(F32)[:, None] * inv[None, :]
    cos, sin = jnp.cos(ang), jnp.sin(ang)
    first = (lane < ROT_HALF)[None, :]
    second = ((lane >= ROT_HALF) & (lane < ROT_DIM))[None, :]
    ta = jnp.where(first | second, cos, 1.0)
    tb = jnp.where(first, -sin, 0.0)
    tc = jnp.where(second, sin, 0.0)
    return ta, tb, tc


def _attn_prompt_kernel(lp_ref, gsub_ref, q_ref, k_ref, vt_ref, o_ref,
                        qst_ref, m_ref, acc_ref, *, lam_init, tq, tk, nh):
    qi = pl.program_id(2)
    comp = lax.broadcasted_iota(jnp.int32, (V_DIM, tq), 0) < HEAD_DIM
    for hh in range(nh):
        qt = q_ref[:, hh * V_DIM:(hh + 1) * V_DIM].astype(F32).T
        qst_ref[hh, :, 0:tq] = jnp.where(comp, qt, 0.0).astype(BF16)
        qst_ref[hh, :, tq:] = jnp.where(comp, 0.0, qt).astype(BF16)
    m_ref[...] = jnp.full(m_ref.shape, NEG, F32)
    acc_ref[...] = jnp.zeros(acc_ref.shape, F32)

    def step(j, mask):
        start = pl.multiple_of(j * tk, tk)
        scores = []
        for hh in range(nh):
            s = _dot(k_ref[pl.ds(start, tk), hh * V_DIM:(hh + 1) * V_DIM], qst_ref[hh])
            scores.append(s if mask is None else jnp.where(mask, s, NEG))
        probs = []
        for hh in range(nh):
            m_old = m_ref[hh]
            m_new = jnp.maximum(m_old, jnp.max(scores[hh], axis=0, keepdims=True))
            m_ref[hh] = m_new
            probs.append((jnp.exp2(m_old - m_new), jnp.exp2(scores[hh] - m_new).astype(BF16)))
        for hh in range(nh):
            alpha, p = probs[hh]
            acc_ref[hh] = alpha * acc_ref[hh] + _dot(vt_ref[j, hh * VT_ROWS:(hh + 1) * VT_ROWS, :], p)

    per = tq // tk

    def body(j, carry):
        step(j, None)
        return carry

    lax.fori_loop(0, qi * per, body, 0)
    row = lax.broadcasted_iota(jnp.int32, (tk, 2 * tq), 0)
    col = lax.broadcasted_iota(jnp.int32, (tk, 2 * tq), 1)
    for d in range(per):
        step(qi * per + d, (d * tk + row) <= (col & (tq - 1)))

    lam = _lam(lp_ref, lam_init)
    for hh in range(nh):
        acc = acc_ref[hh, 0:V_DIM, :]
        l = acc_ref[hh, V_DIM:V_DIM + 1, :]
        ot = acc[:, 0:tq] / l[:, 0:tq] - lam * (acc[:, tq:] / l[:, tq:])
        o_ref[:, hh * V_DIM:(hh + 1) * V_DIM] = _subln(ot.T, gsub_ref[...], lam_init).astype(o_ref.dtype)


def _attn_prompt(q, k, vt, lp, gsub, layer, lam_init, batch, seq):
    tq, tk, nh = Q_TILE, KV_TILE, HEADS_PER_STEP
    nq = seq // tq
    kern = functools.partial(_attn_prompt_kernel, lam_init=lam_init, tq=tq, tk=tk, nh=nh)
    q_spec = pl.BlockSpec((tq, nh * V_DIM), lambda b, h, i: (b * nq + i, h))
    return pl.pallas_call(
        kern,
        grid=(batch, N_HEADS // nh, nq),
        in_specs=[
            _layer_spec(lp, layer, (0, 0)),
            _layer_spec(gsub, layer, (0, 0)),
            q_spec,
            pl.BlockSpec((seq, nh * V_DIM), lambda b, h, i: (b, h)),
            pl.BlockSpec((seq // tk, nh * VT_ROWS, tk), lambda b, h, i: (b, h, 0)),
        ],
        out_specs=q_spec,
        out_shape=jax.ShapeDtypeStruct((batch * seq, ATT_WIDTH), BF16),
        scratch_shapes=[
            pltpu.VMEM((nh, V_DIM, 2 * tq), BF16),
            pltpu.VMEM((nh, 1, 2 * tq), F32),
            pltpu.VMEM((nh, VT_ROWS, 2 * tq), F32),
        ],
        compiler_params=pltpu.CompilerParams(
            dimension_semantics=("arbitrary", "arbitrary", "arbitrary"),
            vmem_limit_bytes=VMEM_LIMIT),
        name="attn_prompt",
    )(lp, gsub, q, k, vt)


def _pool_sample_kernel(hist_ref, u_ref, w_ref, sc_ref, o_ref, *, t_new, pos0):
    ext = [hist_ref[j] for j in range(POOL_HIST)] + [u_ref[t] for t in range(t_new)]
    for g, w in enumerate(POOL_WINDOWS):
        sl = slice(g * POOL_GROUP, (g + 1) * POOL_GROUP)
        rows = []
        for t in range(t_new):
            cur = ext[POOL_HIST + t][:, sl]
            tot = cur
            for j in range(1, w):
                tot = tot + ext[POOL_HIST + t - j][:, sl]
            cnt = float(min(pos0 + t + 1, w))
            rows.append(tot / cnt - cur)
        pooled = jnp.concatenate(rows, axis=0).astype(BF16)
        y = _dot(pooled, w_ref[g]) * sc_ref[:, sl]
        nb = y.shape[0] // t_new
        for t in range(t_new):
            o_ref[t, :, sl] = y[t * nb:(t + 1) * nb]


def _pool_sample(hist_tm, u_tm, w_pool, scale, layer, pos0):
    t_new, nb, _ = u_tm.shape
    kern = functools.partial(_pool_sample_kernel, t_new=t_new, pos0=pos0)
    whole = lambda a: pl.BlockSpec(a.shape, lambda i: (0,) * a.ndim)
    return pl.pallas_call(
        kern,
        grid=(1,),
        in_specs=[whole(hist_tm), whole(u_tm), _layer_spec(w_pool, layer, (0, 0, 0)),
                  _layer_spec(scale, layer, (0, 0))],
        out_specs=pl.BlockSpec((t_new, nb, POOL_WIDTH), lambda i: (0, 0, 0)),
        out_shape=jax.ShapeDtypeStruct((t_new, nb, POOL_WIDTH), F32),
        name="pool_sample",
    )(hist_tm, u_tm, w_pool, scale)


def _sample_attention(pt_ref, lp_ref, gsub_ref, w_ref, kn_ref, vn_ref, ck_hbm, cv_hbm, o_ref,
                      kbuf, vbuf, sem, *, layer, lam_init, t_new, n_seq, n_groups):
    g_pages = PAGES_PER_GROUP
    b = pl.program_id(0)
    w = w_ref[0]
    head_shift = (2 * t_new).bit_length() - 1
    tok_shift = N_HEADS.bit_length() - 1

    def own_head(shape):
        row = lax.broadcasted_iota(jnp.int32, shape, 0)
        col = lax.broadcasted_iota(jnp.int32, shape, 1)
        return (col & (N_HEADS - 1)) == (row >> head_shift), row, col

    def group_copies(seq, grp, slot):
        out = []
        for j in range(g_pages):
            page = pt_ref[seq, grp * g_pages + j]
            rows = pl.ds(j * PAGE_ROWS, PAGE_ROWS)
            out.append(pltpu.make_async_copy(ck_hbm.at[layer, page], kbuf.at[slot, rows], sem.at[slot, 0]))
            out.append(pltpu.make_async_copy(cv_hbm.at[layer, page], vbuf.at[slot, rows], sem.at[slot, 1]))
        return out

    ahead = GROUP_SLOTS - 1

    def first():
        @pl.when(b == 0)
        def _first():
            for g in range(ahead):
                for c in group_copies(0, g, g):
                    c.start()

    def init():
        sn = _dot_nt(w, kn_ref[0])
        own, row, col = own_head(sn.shape)
        valid = own & ((col >> tok_shift) <= (row & (t_new - 1)))
        sn = jnp.where(valid, sn, NEG)
        m0 = jnp.max(sn, axis=1, keepdims=True)
        p0 = jnp.where(valid, jnp.exp2(sn - m0), 0.0)
        return m0, jnp.sum(p0, axis=1, keepdims=True), _dot(p0.astype(BF16), vn_ref[0])

    def group(g, carry):
        m_old, l_old, acc_old = carry
        slot = g % GROUP_SLOTS
        nxt = g + ahead
        if nxt < n_groups:
            for c in group_copies(b, nxt, nxt % GROUP_SLOTS):
                c.start()
        else:
            @pl.when(b + 1 < n_seq)
            def _next_sequence():
                for c in group_copies(b + 1, nxt - n_groups, nxt % GROUP_SLOTS):
                    c.start()

        for c in group_copies(b, g, slot):
            c.wait()

        bias = jnp.where(own_head((w.shape[0], PAGE_ROWS))[0], 0.0, NEG)
        scores = []
        for j in range(g_pages):
            kj = kbuf[slot, j * PAGE_ROWS:(j + 1) * PAGE_ROWS, :].astype(BF16)
            scores.append(_dot_nt(w, kj) + bias)
        top = scores[0]
        for s in scores[1:]:
            top = jnp.maximum(top, s)
        m_new = jnp.maximum(m_old, jnp.max(top, axis=1, keepdims=True))
        alpha = jnp.exp2(m_old - m_new)
        tot = None
        acc = alpha * acc_old
        for j in range(g_pages):
            p = jnp.exp2(scores[j] - m_new)
            tot = p if tot is None else tot + p
            vj = vbuf[slot, j * PAGE_ROWS:(j + 1) * PAGE_ROWS, :].astype(BF16)
            acc = acc + _dot(p.astype(BF16), vj)
        l_new = alpha * l_old + jnp.sum(tot, axis=1, keepdims=True)
        return m_new, l_new, acc

    def finish(carry):
        _, l, acc = carry
        lam = _lam(lp_ref, lam_init)
        accn = acc / l
        per = 2 * t_new
        for hh in range(N_HEADS):
            blk = accn[hh * per:(hh + 1) * per]
            o = blk[0:t_new] - lam * blk[t_new:per]
            o_ref[0, :, hh * V_DIM:(hh + 1) * V_DIM] = _subln(o, gsub_ref[...], lam_init)

    return first, init, group, finish


N_FINISH_IN = 13


def _finish_kernel(*refs, sample):
    if sample is None:
        ins, o_ref = refs[:N_FINISH_IN], refs[N_FINISH_IN]
    else:
        ins = refs[1:1 + N_FINISH_IN]
        o_ref = refs[1 + N_FINISH_IN + 7]
    (x_ref, att_ref, pool_ref, p_ref, wo_ref, gpm_ref, gpf_ref, gqf_ref,
     wg_ref, wu_ref, wd_ref, wpg_ref, wpp_ref) = ins
    v = {}

    def mix_att():
        v["mix"] = _dot(att_ref[...].astype(BF16), wo_ref[0:ATT_WIDTH, :])

    def mix_pool():
        m = v["mix"] + _dot(pool_ref[...].astype(BF16), wo_ref[ATT_WIDTH:, :])
        v["x1"] = x_ref[...] + _rms(m, gpm_ref[...])
        v["h"] = _rms(v["x1"], gpf_ref[...]).astype(BF16)

    def ffn(sl):
        a = (jax.nn.silu(_dot(v["h"], wg_ref[:, sl])) * _dot(v["h"], wu_ref[:, sl])).astype(BF16)
        part = _dot(a, wd_ref[sl, :])
        v["f"] = part + v["f"] if "f" in v else part

    def ple(sl):
        if "x2" not in v:
            v["x2"] = v["x1"] + _rms(v["f"], gqf_ref[...])
            v["x2b"] = v["x2"].astype(BF16)
            v["pb"] = p_ref[...].astype(BF16)
        gate = jax.nn.sigmoid(_dot(v["x2b"], wpg_ref[:, sl]))
        o_ref[:, sl] = v["x2"][:, sl] + gate * _dot(v["pb"], wpp_ref[:, sl])

    chunks = [slice(c, min(c + FF_CHUNK, D_FF)) for c in range(0, D_FF, FF_CHUNK)]
    halves = [slice(0, D_MODEL // 2), slice(D_MODEL // 2, D_MODEL)]
    stages = ([mix_att, mix_pool] + [functools.partial(ffn, sl) for sl in chunks]
              + [functools.partial(ple, sl) for sl in halves])
    if sample is None:
        for stage in stages:
            stage()
        return

    pt_ref = refs[0]
    lp_ref, gsub_ref, wq_ref, kn_ref, vn_ref, ck_hbm, cv_hbm = refs[1 + N_FINISH_IN:1 + N_FINISH_IN + 7]
    os_ref, kbuf, vbuf, sem = refs[1 + N_FINISH_IN + 8:]
    n_groups = sample["n_groups"]
    first, init, group, finish = _sample_attention(
        pt_ref, lp_ref, gsub_ref, wq_ref, kn_ref, vn_ref, ck_hbm, cv_hbm, os_ref, kbuf, vbuf, sem,
        **sample)
    first()
    carry = init()
    done = 0
    for k, stage in enumerate(stages):
        upto = (k + 1) * n_groups // len(stages)
        for g in range(done, upto):
            carry = group(g, carry)
        done = upto
        stage()
    finish(carry)


def _finish(x, att, pool, p, w_o, g_post_mix, g_pre_ffn, g_post_ffn, w_gate, w_up, w_down,
            w_ple_gate, w_ple_proj, layer, tm, sample=None):
    m = x.shape[0]
    row = lambda i, *_: (i, 0)
    resident = lambda a: _layer_spec(a, layer, (0, 0), pipeline_mode=pl.Buffered(1))
    in_specs = [
        pl.BlockSpec((tm, D_MODEL), row),
        pl.BlockSpec((tm, ATT_WIDTH), row),
        pl.BlockSpec((tm, POOL_WIDTH), row),
        pl.BlockSpec((None, tm, PLE_DIM), lambda i, *_: (layer, i, 0)),
        resident(w_o), resident(g_post_mix), resident(g_pre_ffn), resident(g_post_ffn),
        resident(w_gate), resident(w_up), resident(w_down),
        resident(w_ple_gate), resident(w_ple_proj),
    ]
    args = [x, att, pool, p, w_o, g_post_mix, g_pre_ffn, g_post_ffn, w_gate, w_up, w_down,
            w_ple_gate, w_ple_proj]
    assert len(args) == N_FINISH_IN
    out_specs = [pl.BlockSpec((tm, D_MODEL), row)]
    out_shape = [jax.ShapeDtypeStruct((m, D_MODEL), F32)]
    scratch, prefetch, cfg = [], [], None
    if sample is not None:
        page_table, lp, gsub, w_blk, k_new, v_new, cache_k, cache_v, lam_init, t_new = sample
        nb, n_pages = page_table.shape
        n_groups = n_pages // PAGES_PER_GROUP
        assert nb == m // tm and n_pages % PAGES_PER_GROUP == 0 and n_groups % GROUP_SLOTS == 0
        per_seq = lambda i, *_: (i, 0, 0)
        in_specs += [
            _layer_spec(lp, layer, (0, 0)),
            _layer_spec(gsub, layer, (0, 0)),
            pl.BlockSpec((1,) + w_blk.shape[1:], per_seq),
            pl.BlockSpec((1,) + k_new.shape[1:], per_seq),
            pl.BlockSpec((1,) + v_new.shape[1:], per_seq),
            pl.BlockSpec(memory_space=pl.ANY),
            pl.BlockSpec(memory_space=pl.ANY),
        ]
        args += [lp, gsub, w_blk, k_new, v_new, cache_k, cache_v]
        out_specs += [pl.BlockSpec((1, t_new, ATT_WIDTH), per_seq)]
        out_shape += [jax.ShapeDtypeStruct((nb, t_new, ATT_WIDTH), F32)]
        scratch = [
            pltpu.VMEM((GROUP_SLOTS, PAGES_PER_GROUP * PAGE_ROWS, V_DIM), F32),
            pltpu.VMEM((GROUP_SLOTS, PAGES_PER_GROUP * PAGE_ROWS, V_DIM), F32),
            pltpu.SemaphoreType.DMA((GROUP_SLOTS, 2)),
        ]
        prefetch = [page_table]
        cfg = dict(layer=layer, lam_init=lam_init, t_new=t_new, n_seq=nb, n_groups=n_groups)
    grid_spec = pltpu.PrefetchScalarGridSpec(
        num_scalar_prefetch=len(prefetch), grid=(m // tm,), in_specs=in_specs,
        out_specs=out_specs, scratch_shapes=scratch)
    outs = pl.pallas_call(
        functools.partial(_finish_kernel, sample=cfg),
        grid_spec=grid_spec,
        out_shape=out_shape,
        compiler_params=pltpu.CompilerParams(
            dimension_semantics=("arbitrary",), vmem_limit_bytes=VMEM_LIMIT),
        name="finish",
    )(*prefetch, *args)
    return outs[0] if sample is None else outs


def _query_rows(q, nb, t_new):
    qr = q.reshape(nb, t_new, N_HEADS, 2, HEAD_DIM).transpose(0, 2, 3, 1, 4)
    eye = jnp.eye(2, dtype=q.dtype)
    w = qr[:, :, :, :, None, :] * eye[None, None, :, None, :, None]
    return w.reshape(nb, N_HEADS * 2 * t_new, V_DIM)


def kernel(x_prompt, x_sample, p_prompt, p_sample, cache_k, cache_v, state_pool, page_table, w_in, lambda_params, g_sub, w_pool, pool_scale, w_o, g_pre_mix, g_post_mix, g_pre_ffn, g_post_ffn, w_gate, w_up, w_down, w_ple_gate, w_ple_proj):
    batch, seq, _ = x_prompt.shape
    nb, t_new, _ = x_sample.shape
    depth = w_in.shape[0]
    n_pages = page_table.shape[1]
    past_len = n_pages * PAGE_SIZE
    assert seq % ROW_TILE == 0 and seq % Q_TILE == 0
    assert ROW_TILE % KV_TILE == 0 and Q_TILE % KV_TILE == 0 and N_HEADS % HEADS_PER_STEP == 0
    assert t_new & (t_new - 1) == 0 and N_HEADS & (N_HEADS - 1) == 0

    tabs_p = _rope_tables(jnp.arange(seq))
    pos_s = past_len + jnp.arange(t_new)
    tabs_s = tuple(jnp.tile(a, (nb, 1)) for a in _rope_tables(pos_s))
    ck = cache_k.reshape(cache_k.shape[0], cache_k.shape[1], PAGE_ROWS, V_DIM)
    cv = cache_v.reshape(cache_v.shape[0], cache_v.shape[1], PAGE_ROWS, V_DIM)
    bf = lambda a: a.astype(BF16)
    rows = lambda a: a.reshape(depth, 1, -1)
    w_in_b, w_pool_b = bf(w_in), bf(w_pool)
    ffn_w = (bf(w_o), rows(g_post_mix), rows(g_pre_ffn), rows(g_post_ffn),
             bf(w_gate), bf(w_up), bf(w_down), bf(w_ple_gate), bf(w_ple_proj))
    g_in, gsub, scale = rows(g_pre_mix), rows(g_sub), rows(pool_scale)
    pp = p_prompt.reshape(depth, batch * seq, PLE_DIM)
    psm = p_sample.reshape(depth, nb * t_new, PLE_DIM)

    xp = x_prompt.reshape(batch * seq, D_MODEL)
    xs = x_sample.reshape(nb * t_new, D_MODEL)
    kp_rows, vp_rows, pp_rows, ks_rows, vs_rows, ps_rows = [], [], [], [], [], []
    for i in range(depth):
        lam_init = 0.8 - 0.6 * math.exp(-0.3 * i)

        q, kf, vf, kb, vt, pool, tail = _proj(xp, g_in, w_in_b, tabs_p, i, ROW_TILE,
                                              seq // ROW_TILE, (w_pool_b, scale))
        att = _attn_prompt(q, kb, vt, lambda_params, gsub, i, lam_init, batch, seq)
        kp_rows.append(kf.reshape(batch, seq, N_HEADS, V_DIM))
        vp_rows.append(vf.reshape(batch, seq, N_HEADS, V_DIM))
        pp_rows.append(tail.reshape(batch, HIST_PAD, POOL_WIDTH)[:, HIST_PAD - POOL_HIST:])

        q, kf, vf, u = _proj(xs, g_in, w_in_b, tabs_s, i, nb * t_new, 1)
        new_rows = lambda a: bf(a.reshape(nb, t_new * N_HEADS, V_DIM))
        xp, att = _finish(xp, att, pool, pp, *ffn_w, i, ROW_TILE,
                          sample=(page_table, lambda_params, gsub, _query_rows(q, nb, t_new),
                                  new_rows(kf), new_rows(vf), ck, cv, lam_init, t_new))
        u3 = u.reshape(nb, t_new, POOL_WIDTH)
        pool = _pool_sample(state_pool[i].transpose(1, 0, 2), u3.transpose(1, 0, 2), w_pool_b,
                            scale, i, past_len)
        pool = pool.transpose(1, 0, 2).reshape(nb * t_new, POOL_WIDTH)
        xs = _finish(xs, att.reshape(nb * t_new, ATT_WIDTH), pool, psm, *ffn_w, i, nb * t_new)
        ks_rows.append(kf.reshape(nb, t_new, N_HEADS, V_DIM))
        vs_rows.append(vf.reshape(nb, t_new, N_HEADS, V_DIM))
        ps_rows.append(jnp.concatenate([state_pool[i], u3], axis=1)[:, t_new:])

    return (xp.reshape(batch, seq, D_MODEL), xs.reshape(nb, t_new, D_MODEL),
            jnp.stack(kp_rows), jnp.stack(vp_rows), jnp.stack(pp_rows),
            jnp.stack(ks_rows), jnp.stack(vs_rows), jnp.stack(ps_rows))
```

```python
import functools
import math

import jax
import jax.numpy as jnp
from jax import lax
from jax.experimental import pallas as pl
from jax.experimental.pallas import tpu as pltpu

D_MODEL = 1024
N_HEADS = 4
HEAD_DIM = 64
V_DIM = 2 * HEAD_DIM
ATT_WIDTH = N_HEADS * V_DIM
QK_WIDTH = N_HEADS * 2 * HEAD_DIM
ROT_DIM = HEAD_DIM // 4
ROT_HALF = ROT_DIM // 2
ROPE_THETA = 500000.0
ATT_SCALE = HEAD_DIM ** -0.5
Q_SCALE = ATT_SCALE * math.log2(math.e)
POOL_WINDOWS = (2, 4, 8, 16)
N_POOL_GROUPS = len(POOL_WINDOWS)
POOL_WIDTH = D_MODEL - ATT_WIDTH
POOL_GROUP = POOL_WIDTH // N_POOL_GROUPS
POOL_HIST = max(POOL_WINDOWS) - 1
HIST_PAD = POOL_HIST + 1
D_FF = 2816
MXU_DIM = 256
FF_CHUNK = MXU_DIM
PLE_DIM = 256
EPS = 1e-6
NEG = -1e30
PAGE_SIZE = 128
PAGE_ROWS = PAGE_SIZE * N_HEADS

LANES = 128
ROW_TILE = 512
Q_TILE = 512
KV_TILE = 512
ONES_ROWS = 16
VT_ROWS = V_DIM + ONES_ROWS
HEADS_PER_STEP = 4
PAGES_PER_GROUP = 8
GROUP_SLOTS = 4
VMEM_LIMIT = 56 * 1024 * 1024

F32 = jnp.float32
BF16 = jnp.bfloat16


def _dot(a, b):
    return jnp.dot(a, b, preferred_element_type=F32)


def _dot_nt(a, b):
    return lax.dot_general(a, b, (((1,), (1,)), ((), ())), preferred_element_type=F32)


def _rms(v, g):
    return v * lax.rsqrt(jnp.mean(v * v, axis=-1, keepdims=True) + EPS) * g


def _lam(lp_ref, lam_init):
    lp = lp_ref[...]
    s1 = jnp.sum(lp[0:1, :] * lp[1:2, :], axis=1, keepdims=True)
    s2 = jnp.sum(lp[2:3, :] * lp[3:4, :], axis=1, keepdims=True)
    return jnp.exp(s1) - jnp.exp(s2) + lam_init


def _subln(o, gsub, lam_init):
    y = o * lax.rsqrt(jnp.mean(o * o, axis=-1, keepdims=True) + EPS)
    return y * gsub * (1.0 - lam_init)


def _proj_kernel(*refs, tm, tiles_per_seq, for_prompt):
    if for_prompt:
        (x_ref, g_ref, w_ref, ta_ref, tb_ref, tc_ref, wp_ref, sc_ref,
         q_ref, kf_ref, vf_ref, kb_ref, vt_ref, pool_ref, tail_ref, ext_ref) = refs
    else:
        x_ref, g_ref, w_ref, ta_ref, tb_ref, tc_ref, q_ref, kf_ref, vf_ref, u_ref = refs
    h = _rms(x_ref[...], g_ref[...]).astype(BF16)
    ta, tb, tc = ta_ref[...], tb_ref[...], tc_ref[...]

    def rope(z):
        return z * ta + pltpu.roll(z, LANES - ROT_HALF, 1) * tb + pltpu.roll(z, ROT_HALF, 1) * tc

    def head_rows(hh):
        return pl.ds(hh, tm, stride=N_HEADS)

    zq = _dot(h, w_ref[:, 0:QK_WIDTH])
    for hh in range(N_HEADS):
        sl = slice(hh * V_DIM, (hh + 1) * V_DIM)
        q_ref[:, sl] = (rope(zq[:, sl]) * Q_SCALE).astype(BF16)
    zk = _dot(h, w_ref[:, QK_WIDTH:2 * QK_WIDTH])
    for hh in range(N_HEADS):
        sl = slice(hh * V_DIM, (hh + 1) * V_DIM)
        k = rope(zk[:, sl])
        kf_ref[head_rows(hh), :] = k
        if for_prompt:
            kb_ref[:, sl] = k.astype(BF16)
    zv = _dot(h, w_ref[:, 2 * QK_WIDTH:2 * QK_WIDTH + ATT_WIDTH])
    for hh in range(N_HEADS):
        vf_ref[head_rows(hh), :] = zv[:, hh * V_DIM:(hh + 1) * V_DIM]
    u = _dot(h, w_ref[:, 2 * QK_WIDTH + ATT_WIDTH:])
    if not for_prompt:
        u_ref[...] = u
    else:
        ones = jnp.ones((ONES_ROWS, KV_TILE), BF16)
        for c in range(tm // KV_TILE):
            vt = zv[c * KV_TILE:(c + 1) * KV_TILE, :].T.astype(BF16)
            for hh in range(N_HEADS):
                vt_ref[c, hh * VT_ROWS:hh * VT_ROWS + V_DIM, :] = vt[hh * V_DIM:(hh + 1) * V_DIM]
                vt_ref[c, hh * VT_ROWS + V_DIM:(hh + 1) * VT_ROWS, :] = ones
        _pool_rows(u, ext_ref, wp_ref, sc_ref, pool_ref, tail_ref, tm, tiles_per_seq)


def _pool_rows(u, ext_ref, w_ref, sc_ref, o_ref, tail_ref, tm, tiles_per_seq):
    it = pl.program_id(0) % tiles_per_seq

    @pl.when(it == 0)
    def _zero_history():
        ext_ref[0:HIST_PAD, :] = jnp.zeros((HIST_PAD, POOL_WIDTH), F32)

    @pl.when(it != 0)
    def _carry_history():
        ext_ref[0:HIST_PAD, :] = ext_ref[tm:tm + HIST_PAD, :]

    ext_ref[HIST_PAD:, :] = u
    tail_ref[...] = u[tm - HIST_PAD:, :]
    pos = it * tm + lax.broadcasted_iota(jnp.int32, (tm, POOL_GROUP), 0)
    for g, w in enumerate(POOL_WINDOWS):
        sl = slice(g * POOL_GROUP, (g + 1) * POOL_GROUP)
        cur = u[:, sl]
        tot = cur
        for j in range(1, w):
            tot = tot + ext_ref[HIST_PAD - j:HIST_PAD - j + tm, sl]
        cnt = jnp.minimum(pos + 1, w).astype(F32)
        pooled = (tot / cnt - cur).astype(BF16)
        o_ref[:, sl] = (_dot(pooled, w_ref[g]) * sc_ref[:, sl]).astype(o_ref.dtype)


def _layer_spec(a, layer, index_map_tail, **kw):
    return pl.BlockSpec((None,) + a.shape[1:], lambda *_: (layer,) + index_map_tail, **kw)


def _proj(x, g, w_in, tabs, layer, tm, tiles_per_seq, pool_w=None):
    m = x.shape[0]
    for_prompt = pool_w is not None
    row = lambda i: (i, 0)
    tab = lambda i: (i % tiles_per_seq, 0)
    wide = pl.BlockSpec((tm, ATT_WIDTH), row)
    head_major = pl.BlockSpec((tm * N_HEADS, V_DIM), row)
    rows_f32 = jax.ShapeDtypeStruct((m * N_HEADS, V_DIM), F32)
    in_specs = [
        pl.BlockSpec((tm, D_MODEL), row),
        _layer_spec(g, layer, (0, 0)),
        _layer_spec(w_in, layer, (0, 0)),
        pl.BlockSpec((tm, LANES), tab),
        pl.BlockSpec((tm, LANES), tab),
        pl.BlockSpec((tm, LANES), tab),
    ]
    args = [x, g, w_in, *tabs]
    out_specs = [wide, head_major, head_major]
    out_shape = [jax.ShapeDtypeStruct((m, QK_WIDTH), BF16), rows_f32, rows_f32]
    scratch = []
    if for_prompt:
        w_pool, scale = pool_w
        in_specs += [_layer_spec(w_pool, layer, (0, 0, 0)), _layer_spec(scale, layer, (0, 0))]
        args += [w_pool, scale]
        per = tm // KV_TILE
        out_specs += [wide,
                      pl.BlockSpec((per, N_HEADS * VT_ROWS, KV_TILE), lambda i: (i, 0, 0)),
                      wide,
                      pl.BlockSpec((HIST_PAD, POOL_WIDTH), lambda i: (i // tiles_per_seq, 0))]
        out_shape += [jax.ShapeDtypeStruct((m, QK_WIDTH), BF16),
                      jax.ShapeDtypeStruct((m // KV_TILE, N_HEADS * VT_ROWS, KV_TILE), BF16),
                      jax.ShapeDtypeStruct((m, POOL_WIDTH), BF16),
                      jax.ShapeDtypeStruct((m // tiles_per_seq // tm * HIST_PAD, POOL_WIDTH), F32)]
        scratch = [pltpu.VMEM((tm + HIST_PAD, POOL_WIDTH), F32)]
    else:
        out_specs += [wide]
        out_shape += [jax.ShapeDtypeStruct((m, POOL_WIDTH), F32)]
    return pl.pallas_call(
        functools.partial(_proj_kernel, tm=tm, tiles_per_seq=tiles_per_seq, for_prompt=for_prompt),
        grid=(m // tm,),
        in_specs=in_specs,
        out_specs=out_specs,
        out_shape=out_shape,
        scratch_shapes=scratch,
        compiler_params=pltpu.CompilerParams(
            dimension_semantics=("arbitrary",), vmem_limit_bytes=VMEM_LIMIT),
        name="proj",
    )(*args)


def _rope_tables(pos):
    lane = jnp.arange(LANES) % HEAD_DIM
    inv = jnp.power(ROPE_THETA, -(2 * (lane % ROT_HALF)).astype(F32) / ROT_DIM)
    ang = pos.astype(F32)[:, None] * inv[None, :]
    cos, sin = jnp.cos(ang), jnp.sin(ang)
    first = (lane < ROT_HALF)[None, :]
    second = ((lane >= ROT_HALF) & (lane < ROT_DIM))[None, :]
    ta = jnp.where(first | second, cos, 1.0)
    tb = jnp.where(first, -sin, 0.0)
    tc = jnp.where(second, sin, 0.0)
    return ta, tb, tc


def _attn_prompt_kernel(*refs, lam_init, tile, nh, n_stack):
    lp_ref, gsub_ref, q_ref, k_ref, vt_ref = refs[:5]
    rows_hbm = refs[5:5 + n_stack]
    o_ref = refs[5 + n_stack]
    stacked_hbm = refs[6 + n_stack:6 + n_stack + (2 if n_stack else 0)]
    qst_ref, m_ref, acc_ref = refs[6 + n_stack + len(stacked_hbm):][:3]
    qi = pl.program_id(2)
    if n_stack:
        sem = refs[-1]
        depth = n_stack // 2
        copies = [pltpu.make_async_copy(rows_hbm[t * depth + d], stacked_hbm[t].at[d], sem.at[t * depth + d])
                  for t in range(2) for d in range(depth)]
        step_id = (pl.program_id(0) * pl.num_programs(1) + pl.program_id(1)) * pl.num_programs(2) + qi
        n_steps = pl.num_programs(0) * pl.num_programs(1) * pl.num_programs(2)

        @pl.when(step_id == 0)
        def _start_stack():
            for c in copies:
                c.start()

    half = tile // 2
    comp = lax.broadcasted_iota(jnp.int32, (V_DIM, half), 0) < HEAD_DIM
    for hh in range(nh):
        qt = q_ref[:, hh * V_DIM:(hh + 1) * V_DIM].astype(F32).T
        for part in range(2):
            qp = qt[:, part * half:(part + 1) * half]
            qst_ref[hh, :, 2 * part * half:(2 * part + 1) * half] = jnp.where(comp, qp, 0.0).astype(BF16)
            qst_ref[hh, :, (2 * part + 1) * half:(2 * part + 2) * half] = jnp.where(comp, 0.0, qp).astype(BF16)
    m_ref[...] = jnp.full(m_ref.shape, NEG, F32)
    acc_ref[...] = jnp.zeros(acc_ref.shape, F32)

    def step(j, k_lo, k_n, c_lo, c_n, mask):
        rows = pl.ds(pl.multiple_of(j * tile + k_lo, half), k_n)
        cols = slice(c_lo, c_lo + c_n)
        scores = []
        for hh in range(nh):
            s = _dot(k_ref[rows, hh * V_DIM:(hh + 1) * V_DIM], qst_ref[hh, :, cols])
            scores.append(s if mask is None else jnp.where(mask, s, NEG))
        probs = []
        for hh in range(nh):
            m_old = m_ref[hh, :, cols]
            m_new = jnp.maximum(m_old, jnp.max(scores[hh], axis=0, keepdims=True))
            m_ref[hh, :, cols] = m_new
            probs.append((jnp.exp2(m_old - m_new), jnp.exp2(scores[hh] - m_new).astype(BF16)))
        for hh in range(nh):
            alpha, p = probs[hh]
            vt = vt_ref[j, hh * VT_ROWS:(hh + 1) * VT_ROWS, k_lo:k_lo + k_n]
            acc_ref[hh, :, cols] = alpha * acc_ref[hh, :, cols] + _dot(vt, p)

    def body(j, carry):
        step(j, 0, tile, 0, 2 * tile, None)
        return carry

    lax.fori_loop(0, qi, body, 0)
    row = lax.broadcasted_iota(jnp.int32, (half, 2 * tile), 0)
    col = lax.broadcasted_iota(jnp.int32, (half, 2 * tile), 1)
    causal = row <= (col & (half - 1))
    step(qi, 0, half, 0, 2 * tile, causal | (col >= tile))
    step(qi, half, half, tile, tile, causal[:, 0:tile])

    lam = _lam(lp_ref, lam_init)
    for hh in range(nh):
        acc = acc_ref[hh, 0:V_DIM, :]
        l = acc_ref[hh, V_DIM:V_DIM + 1, :]
        on = acc / l
        for part in range(2):
            c1 = slice(2 * part * half, (2 * part + 1) * half)
            c2 = slice((2 * part + 1) * half, (2 * part + 2) * half)
            ot = on[:, c1] - lam * on[:, c2]
            o_ref[part * half:(part + 1) * half, hh * V_DIM:(hh + 1) * V_DIM] = _subln(
                ot.T, gsub_ref[...], lam_init).astype(o_ref.dtype)

    if n_stack:
        @pl.when(step_id == n_steps - 1)
        def _wait_stack():
            for c in copies:
                c.wait()


def _attn_prompt(q, k, vt, lp, gsub, layer, lam_init, batch, seq, stack_rows=()):
    tq, tk, nh = Q_TILE, KV_TILE, HEADS_PER_STEP
    assert tq == tk
    nq = seq // tq
    n_stack = len(stack_rows)
    kern = functools.partial(_attn_prompt_kernel, lam_init=lam_init, tile=tq, nh=nh, n_stack=n_stack)
    q_spec = pl.BlockSpec((tq, nh * V_DIM), lambda b, h, i: (b * nq + i, h))
    hbm = pl.BlockSpec(memory_space=pl.ANY)
    out_specs = [q_spec]
    out_shape = [jax.ShapeDtypeStruct((batch * seq, ATT_WIDTH), BF16)]
    scratch = [
        pltpu.VMEM((nh, V_DIM, 2 * tq), BF16),
        pltpu.VMEM((nh, 1, 2 * tq), F32),
        pltpu.VMEM((nh, VT_ROWS, 2 * tq), F32),
    ]
    if n_stack:
        stacked = jax.ShapeDtypeStruct((n_stack // 2,) + stack_rows[0].shape, stack_rows[0].dtype)
        out_specs += [hbm, hbm]
        out_shape += [stacked, stacked]
        scratch += [pltpu.SemaphoreType.DMA((n_stack,))]
    outs = pl.pallas_call(
        kern,
        grid=(batch, N_HEADS // nh, nq),
        in_specs=[
            _layer_spec(lp, layer, (0, 0)),
            _layer_spec(gsub, layer, (0, 0)),
            q_spec,
            pl.BlockSpec((seq, nh * V_DIM), lambda b, h, i: (b, h)),
            pl.BlockSpec((seq // tk, nh * VT_ROWS, tk), lambda b, h, i: (b, h, 0)),
        ] + [hbm] * n_stack,
        out_specs=out_specs,
        out_shape=out_shape,
        scratch_shapes=scratch,
        compiler_params=pltpu.CompilerParams(
            dimension_semantics=("arbitrary", "arbitrary", "arbitrary"),
            vmem_limit_bytes=VMEM_LIMIT),
        name="attn_prompt",
    )(lp, gsub, q, k, vt, *stack_rows)
    return outs if n_stack else outs[0]


def _pool_sample_kernel(hist_ref, u_ref, w_ref, sc_ref, o_ref, *, t_new, pos0):
    ext = [hist_ref[j] for j in range(POOL_HIST)] + [u_ref[t] for t in range(t_new)]
    for g, w in enumerate(POOL_WINDOWS):
        sl = slice(g * POOL_GROUP, (g + 1) * POOL_GROUP)
        rows = []
        for t in range(t_new):
            cur = ext[POOL_HIST + t][:, sl]
            tot = cur
            for j in range(1, w):
                tot = tot + ext[POOL_HIST + t - j][:, sl]
            cnt = float(min(pos0 + t + 1, w))
            rows.append(tot / cnt - cur)
        pooled = jnp.concatenate(rows, axis=0).astype(BF16)
        y = _dot(pooled, w_ref[g]) * sc_ref[:, sl]
        nb = y.shape[0] // t_new
        for t in range(t_new):
            o_ref[t, :, sl] = y[t * nb:(t + 1) * nb]


def _pool_sample(hist_tm, u_tm, w_pool, scale, layer, pos0):
    t_new, nb, _ = u_tm.shape
    kern = functools.partial(_pool_sample_kernel, t_new=t_new, pos0=pos0)
    whole = lambda a: pl.BlockSpec(a.shape, lambda i: (0,) * a.ndim)
    return pl.pallas_call(
        kern,
        grid=(1,),
        in_specs=[whole(hist_tm), whole(u_tm), _layer_spec(w_pool, layer, (0, 0, 0)),
                  _layer_spec(scale, layer, (0, 0))],
        out_specs=pl.BlockSpec((t_new, nb, POOL_WIDTH), lambda i: (0, 0, 0)),
        out_shape=jax.ShapeDtypeStruct((t_new, nb, POOL_WIDTH), F32),
        name="pool_sample",
    )(hist_tm, u_tm, w_pool, scale)


def _sample_attention(pt_ref, lp_ref, gsub_ref, w_ref, kn_ref, vn_ref, ck_hbm, cv_hbm, o_ref,
                      kbuf, vbuf, sem, *, layer, lam_init, t_new, n_seq, n_groups):
    g_pages = PAGES_PER_GROUP
    b = pl.program_id(0)
    w = w_ref[0]
    head_shift = (2 * t_new).bit_length() - 1
    tok_shift = N_HEADS.bit_length() - 1

    def own_head(shape):
        row = lax.broadcasted_iota(jnp.int32, shape, 0)
        col = lax.broadcasted_iota(jnp.int32, shape, 1)
        return (col & (N_HEADS - 1)) == (row >> head_shift), row, col

    def group_copies(seq, grp, slot):
        out = []
        for j in range(g_pages):
            page = pt_ref[seq, grp * g_pages + j]
            rows = pl.ds(j * PAGE_ROWS, PAGE_ROWS)
            out.append(pltpu.make_async_copy(ck_hbm.at[layer, page], kbuf.at[slot, rows], sem.at[slot, 0]))
            out.append(pltpu.make_async_copy(cv_hbm.at[layer, page], vbuf.at[slot, rows], sem.at[slot, 1]))
        return out

    ahead = GROUP_SLOTS - 1

    def first():
        @pl.when(b == 0)
        def _first():
            for g in range(ahead):
                for c in group_copies(0, g, g):
                    c.start()

    def init():
        sn = _dot_nt(w, kn_ref[0])
        own, row, col = own_head(sn.shape)
        valid = own & ((col >> tok_shift) <= (row & (t_new - 1)))
        sn = jnp.where(valid, sn, NEG)
        m0 = jnp.max(sn, axis=1, keepdims=True)
        p0 = jnp.where(valid, jnp.exp2(sn - m0), 0.0)
        return m0, jnp.sum(p0, axis=1, keepdims=True), _dot(p0.astype(BF16), vn_ref[0])

    def group(g, carry):
        m_old, l_old, acc_old = carry
        slot = g % GROUP_SLOTS
        nxt = g + ahead
        if nxt < n_groups:
            for c in group_copies(b, nxt, nxt % GROUP_SLOTS):
                c.start()
        else:
            @pl.when(b + 1 < n_seq)
            def _next_sequence():
                for c in group_copies(b + 1, nxt - n_groups, nxt % GROUP_SLOTS):
                    c.start()

        for c in group_copies(b, g, slot):
            c.wait()

        bias = jnp.where(own_head((w.shape[0], PAGE_ROWS))[0], 0.0, NEG)
        scores = []
        for j in range(g_pages):
            kj = kbuf[slot, j * PAGE_ROWS:(j + 1) * PAGE_ROWS, :].astype(BF16)
            scores.append(_dot_nt(w, kj) + bias)
        top = scores[0]
        for s in scores[1:]:
            top = jnp.maximum(top, s)
        m_new = jnp.maximum(m_old, jnp.max(top, axis=1, keepdims=True))
        alpha = jnp.exp2(m_old - m_new)
        tot = None
        acc = alpha * acc_old
        for j in range(g_pages):
            p = jnp.exp2(scores[j] - m_new)
            tot = p if tot is None else tot + p
            vj = vbuf[slot, j * PAGE_ROWS:(j + 1) * PAGE_ROWS, :].astype(BF16)
            acc = acc + _dot(p.astype(BF16), vj)
        l_new = alpha * l_old + jnp.sum(tot, axis=1, keepdims=True)
        return m_new, l_new, acc

    def finish(carry):
        _, l, acc = carry
        lam = _lam(lp_ref, lam_init)
        accn = acc / l
        per = 2 * t_new
        for hh in range(N_HEADS):
            blk = accn[hh * per:(hh + 1) * per]
            o = blk[0:t_new] - lam * blk[t_new:per]
            o_ref[0, :, hh * V_DIM:(hh + 1) * V_DIM] = _subln(o, gsub_ref[...], lam_init)

    return first, init, group, finish


N_FINISH_IN = 13


def _finish_kernel(*refs, sample):
    if sample is None:
        ins, o_ref = refs[:N_FINISH_IN], refs[N_FINISH_IN]
    else:
        ins = refs[1:1 + N_FINISH_IN]
        o_ref = refs[1 + N_FINISH_IN + 7]
    (x_ref, att_ref, pool_ref, p_ref, wo_ref, gpm_ref, gpf_ref, gqf_ref,
     wg_ref, wu_ref, wd_ref, wpg_ref, wpp_ref) = ins
    v = {}

    def mix_att():
        v["mix"] = _dot(att_ref[...].astype(BF16), wo_ref[0:ATT_WIDTH, :])

    def mix_pool():
        m = v["mix"] + _dot(pool_ref[...].astype(BF16), wo_ref[ATT_WIDTH:, :])
        v["x1"] = x_ref[...] + _rms(m, gpm_ref[...])
        v["h"] = _rms(v["x1"], gpf_ref[...]).astype(BF16)

    def ffn(sl):
        a = (jax.nn.silu(_dot(v["h"], wg_ref[:, sl])) * _dot(v["h"], wu_ref[:, sl])).astype(BF16)
        part = _dot(a, wd_ref[sl, :])
        v["f"] = part + v["f"] if "f" in v else part

    def ple(sl):
        if "x2" not in v:
            v["x2"] = v["x1"] + _rms(v["f"], gqf_ref[...])
            v["x2b"] = v["x2"].astype(BF16)
            v["pb"] = p_ref[...].astype(BF16)
        gate = jax.nn.sigmoid(_dot(v["x2b"], wpg_ref[:, sl]))
        o_ref[:, sl] = v["x2"][:, sl] + gate * _dot(v["pb"], wpp_ref[:, sl])

    chunks = [slice(c, min(c + FF_CHUNK, D_FF)) for c in range(0, D_FF, FF_CHUNK)]
    halves = [slice(0, D_MODEL // 2), slice(D_MODEL // 2, D_MODEL)]
    stages = ([mix_att, mix_pool] + [functools.partial(ffn, sl) for sl in chunks]
              + [functools.partial(ple, sl) for sl in halves])
    if sample is None:
        for stage in stages:
            stage()
        return

    pt_ref = refs[0]
    lp_ref, gsub_ref, wq_ref, kn_ref, vn_ref, ck_hbm, cv_hbm = refs[1 + N_FINISH_IN:1 + N_FINISH_IN + 7]
    os_ref, kbuf, vbuf, sem = refs[1 + N_FINISH_IN + 8:]
    n_groups = sample["n_groups"]
    first, init, group, finish = _sample_attention(
        pt_ref, lp_ref, gsub_ref, wq_ref, kn_ref, vn_ref, ck_hbm, cv_hbm, os_ref, kbuf, vbuf, sem,
        **sample)
    first()
    carry = init()
    done = 0
    for k, stage in enumerate(stages):
        upto = (k + 1) * n_groups // len(stages)
        for g in range(done, upto):
            carry = group(g, carry)
        done = upto
        stage()
    finish(carry)


def _finish(x, att, pool, p, w_o, g_post_mix, g_pre_ffn, g_post_ffn, w_gate, w_up, w_down,
            w_ple_gate, w_ple_proj, layer, tm, sample=None):
    m = x.shape[0]
    row = lambda i, *_: (i, 0)
    resident = lambda a: _layer_spec(a, layer, (0, 0), pipeline_mode=pl.Buffered(1))
    in_specs = [
        pl.BlockSpec((tm, D_MODEL), row),
        pl.BlockSpec((tm, ATT_WIDTH), row),
        pl.BlockSpec((tm, POOL_WIDTH), row),
        pl.BlockSpec((None, tm, PLE_DIM), lambda i, *_: (layer, i, 0)),
        resident(w_o), resident(g_post_mix), resident(g_pre_ffn), resident(g_post_ffn),
        resident(w_gate), resident(w_up), resident(w_down),
        resident(w_ple_gate), resident(w_ple_proj),
    ]
    args = [x, att, pool, p, w_o, g_post_mix, g_pre_ffn, g_post_ffn, w_gate, w_up, w_down,
            w_ple_gate, w_ple_proj]
    assert len(args) == N_FINISH_IN
    out_specs = [pl.BlockSpec((tm, D_MODEL), row)]
    out_shape = [jax.ShapeDtypeStruct((m, D_MODEL), F32)]
    scratch, prefetch, cfg = [], [], None
    if sample is not None:
        page_table, lp, gsub, w_blk, k_new, v_new, cache_k, cache_v, lam_init, t_new = sample
        nb, n_pages = page_table.shape
        n_groups = n_pages // PAGES_PER_GROUP
        assert nb == m // tm and n_pages % PAGES_PER_GROUP == 0 and n_groups % GROUP_SLOTS == 0
        per_seq = lambda i, *_: (i, 0, 0)
        in_specs += [
            _layer_spec(lp, layer, (0, 0)),
            _layer_spec(gsub, layer, (0, 0)),
            pl.BlockSpec((1,) + w_blk.shape[1:], per_seq),
            pl.BlockSpec((1,) + k_new.shape[1:], per_seq),
            pl.BlockSpec((1,) + v_new.shape[1:], per_seq),
            pl.BlockSpec(memory_space=pl.ANY),
            pl.BlockSpec(memory_space=pl.ANY),
        ]
        args += [lp, gsub, w_blk, k_new, v_new, cache_k, cache_v]
        out_specs += [pl.BlockSpec((1, t_new, ATT_WIDTH), per_seq)]
        out_shape += [jax.ShapeDtypeStruct((nb, t_new, ATT_WIDTH), F32)]
        scratch = [
            pltpu.VMEM((GROUP_SLOTS, PAGES_PER_GROUP * PAGE_ROWS, V_DIM), F32),
            pltpu.VMEM((GROUP_SLOTS, PAGES_PER_GROUP * PAGE_ROWS, V_DIM), F32),
            pltpu.SemaphoreType.DMA((GROUP_SLOTS, 2)),
        ]
        prefetch = [page_table]
        cfg = dict(layer=layer, lam_init=lam_init, t_new=t_new, n_seq=nb, n_groups=n_groups)
    grid_spec = pltpu.PrefetchScalarGridSpec(
        num_scalar_prefetch=len(prefetch), grid=(m // tm,), in_specs=in_specs,
        out_specs=out_specs, scratch_shapes=scratch)
    outs = pl.pallas_call(
        functools.partial(_finish_kernel, sample=cfg),
        grid_spec=grid_spec,
        out_shape=out_shape,
        compiler_params=pltpu.CompilerParams(
            dimension_semantics=("arbitrary",), vmem_limit_bytes=VMEM_LIMIT),
        name="finish",
    )(*prefetch, *args)
    return outs[0] if sample is None else outs


def _query_rows(q, nb, t_new):
    qr = q.reshape(nb, t_new, N_HEADS, 2, HEAD_DIM).transpose(0, 2, 3, 1, 4)
    eye = jnp.eye(2, dtype=q.dtype)
    w = qr[:, :, :, :, None, :] * eye[None, None, :, None, :, None]
    return w.reshape(nb, N_HEADS * 2 * t_new, V_DIM)


def kernel(x_prompt, x_sample, p_prompt, p_sample, cache_k, cache_v, state_pool, page_table, w_in, lambda_params, g_sub, w_pool, pool_scale, w_o, g_pre_mix, g_post_mix, g_pre_ffn, g_post_ffn, w_gate, w_up, w_down, w_ple_gate, w_ple_proj):
    batch, seq, _ = x_prompt.shape
    nb, t_new, _ = x_sample.shape
    depth = w_in.shape[0]
    n_pages = page_table.shape[1]
    past_len = n_pages * PAGE_SIZE
    assert seq % ROW_TILE == 0 and seq % Q_TILE == 0
    assert ROW_TILE % KV_TILE == 0 and Q_TILE % KV_TILE == 0 and N_HEADS % HEADS_PER_STEP == 0
    assert t_new & (t_new - 1) == 0 and N_HEADS & (N_HEADS - 1) == 0

    tabs_p = _rope_tables(jnp.arange(seq))
    pos_s = past_len + jnp.arange(t_new)
    tabs_s = tuple(jnp.tile(a, (nb, 1)) for a in _rope_tables(pos_s))
    ck = cache_k.reshape(cache_k.shape[0], cache_k.shape[1], PAGE_ROWS, V_DIM)
    cv = cache_v.reshape(cache_v.shape[0], cache_v.shape[1], PAGE_ROWS, V_DIM)
    bf = lambda a: a.astype(BF16)
    rows = lambda a: a.reshape(depth, 1, -1)
    w_in_b, w_pool_b = bf(w_in), bf(w_pool)
    ffn_w = (bf(w_o), rows(g_post_mix), rows(g_pre_ffn), rows(g_post_ffn),
             bf(w_gate), bf(w_up), bf(w_down), bf(w_ple_gate), bf(w_ple_proj))
    g_in, gsub, scale = rows(g_pre_mix), rows(g_sub), rows(pool_scale)
    pp = p_prompt.reshape(depth, batch * seq, PLE_DIM)
    psm = p_sample.reshape(depth, nb * t_new, PLE_DIM)

    xp = x_prompt.reshape(batch * seq, D_MODEL)
    xs = x_sample.reshape(nb * t_new, D_MODEL)
    kp_rows, vp_rows, pp_rows, ks_rows, vs_rows, ps_rows = [], [], [], [], [], []
    for i in range(depth):
        lam_init = 0.8 - 0.6 * math.exp(-0.3 * i)

        q, kf, vf, kb, vt, pool, tail = _proj(xp, g_in, w_in_b, tabs_p, i, ROW_TILE,
                                              seq // ROW_TILE, (w_pool_b, scale))
        kp_rows.append(kf)
        vp_rows.append(vf)
        if i + 1 < depth:
            att = _attn_prompt(q, kb, vt, lambda_params, gsub, i, lam_init, batch, seq)
        else:
            att, k_prompt, v_prompt = _attn_prompt(q, kb, vt, lambda_params, gsub, i, lam_init,
                                                   batch, seq, stack_rows=(*kp_rows, *vp_rows))
        pp_rows.append(tail.reshape(batch, HIST_PAD, POOL_WIDTH)[:, HIST_PAD - POOL_HIST:])

        q, kf, vf, u = _proj(xs, g_in, w_in_b, tabs_s, i, nb * t_new, 1)
        new_rows = lambda a: bf(a.reshape(nb, t_new * N_HEADS, V_DIM))
        xp, att = _finish(xp, att, pool, pp, *ffn_w, i, ROW_TILE,
                          sample=(page_table, lambda_params, gsub, _query_rows(q, nb, t_new),
                                  new_rows(kf), new_rows(vf), ck, cv, lam_init, t_new))
        u3 = u.reshape(nb, t_new, POOL_WIDTH)
        pool = _pool_sample(state_pool[i].transpose(1, 0, 2), u3.transpose(1, 0, 2), w_pool_b,
                            scale, i, past_len)
        pool = pool.transpose(1, 0, 2).reshape(nb * t_new, POOL_WIDTH)
        xs = _finish(xs, att.reshape(nb * t_new, ATT_WIDTH), pool, psm, *ffn_w, i, nb * t_new)
        ks_rows.append(kf.reshape(nb, t_new, N_HEADS, V_DIM))
        vs_rows.append(vf.reshape(nb, t_new, N_HEADS, V_DIM))
        ps_rows.append(jnp.concatenate([state_pool[i], u3], axis=1)[:, t_new:])

    return (xp.reshape(batch, seq, D_MODEL), xs.reshape(nb, t_new, D_MODEL),
            k_prompt.reshape(depth, batch, seq, N_HEADS, V_DIM),
            v_prompt.reshape(depth, batch, seq, N_HEADS, V_DIM), jnp.stack(pp_rows),
            jnp.stack(ks_rows), jnp.stack(vs_rows), jnp.stack(ps_rows))
```

```python
import functools
import math

import jax
import jax.numpy as jnp
from jax import lax
from jax.experimental import pallas as pl
from jax.experimental.pallas import tpu as pltpu

D_MODEL = 1024
N_HEADS = 4
HEAD_DIM = 64
V_DIM = 2 * HEAD_DIM
ATT_WIDTH = N_HEADS * V_DIM
QK_WIDTH = N_HEADS * 2 * HEAD_DIM
ROT_DIM = HEAD_DIM // 4
ROT_HALF = ROT_DIM // 2
ROPE_THETA = 500000.0
ATT_SCALE = HEAD_DIM ** -0.5
Q_SCALE = ATT_SCALE * math.log2(math.e)
POOL_WINDOWS = (2, 4, 8, 16)
N_POOL_GROUPS = len(POOL_WINDOWS)
POOL_WIDTH = D_MODEL - ATT_WIDTH
POOL_GROUP = POOL_WIDTH // N_POOL_GROUPS
POOL_HIST = max(POOL_WINDOWS) - 1
HIST_PAD = POOL_HIST + 1
D_FF = 2816
MXU_DIM = 256
FF_CHUNK = MXU_DIM
PLE_DIM = 256
EPS = 1e-6
NEG = -1e30
PAGE_SIZE = 128
PAGE_ROWS = PAGE_SIZE * N_HEADS

LANES = 128
ROW_TILE = 512
Q_TILE = 512
KV_TILE = 512
ONES_ROWS = 16
VT_ROWS = V_DIM + ONES_ROWS
HEADS_PER_STEP = 4
PAGES_PER_GROUP = 8
GROUP_SLOTS = 4
STACK_CHUNKS = 32
VMEM_LIMIT = 56 * 1024 * 1024

F32 = jnp.float32
BF16 = jnp.bfloat16


def _dot(a, b):
    return jnp.dot(a, b, preferred_element_type=F32)


def _dot_nt(a, b):
    return lax.dot_general(a, b, (((1,), (1,)), ((), ())), preferred_element_type=F32)


def _rms(v, g):
    return v * lax.rsqrt(jnp.mean(v * v, axis=-1, keepdims=True) + EPS) * g


def _lam(lp_ref, lam_init):
    lp = lp_ref[...]
    s1 = jnp.sum(lp[0:1, :] * lp[1:2, :], axis=1, keepdims=True)
    s2 = jnp.sum(lp[2:3, :] * lp[3:4, :], axis=1, keepdims=True)
    return jnp.exp(s1) - jnp.exp(s2) + lam_init


def _subln(o, gsub, lam_init):
    y = o * lax.rsqrt(jnp.mean(o * o, axis=-1, keepdims=True) + EPS)
    return y * gsub * (1.0 - lam_init)


def _proj_kernel(*refs, tm, tiles_per_seq, for_prompt):
    if for_prompt:
        (x_ref, g_ref, w_ref, ta_ref, tb_ref, tc_ref, wp_ref, sc_ref,
         q_ref, kf_ref, vf_ref, kb_ref, vt_ref, pool_ref, tail_ref, ext_ref) = refs
    else:
        x_ref, g_ref, w_ref, ta_ref, tb_ref, tc_ref, q_ref, kf_ref, vf_ref, u_ref = refs
    h = _rms(x_ref[...], g_ref[...]).astype(BF16)
    ta, tb, tc = ta_ref[...], tb_ref[...], tc_ref[...]

    def rope(z):
        return z * ta + pltpu.roll(z, LANES - ROT_HALF, 1) * tb + pltpu.roll(z, ROT_HALF, 1) * tc

    def head_rows(hh):
        return pl.ds(hh, tm, stride=N_HEADS)

    zq = _dot(h, w_ref[:, 0:QK_WIDTH])
    for hh in range(N_HEADS):
        sl = slice(hh * V_DIM, (hh + 1) * V_DIM)
        q_ref[:, sl] = (rope(zq[:, sl]) * Q_SCALE).astype(BF16)
    zk = _dot(h, w_ref[:, QK_WIDTH:2 * QK_WIDTH])
    for hh in range(N_HEADS):
        sl = slice(hh * V_DIM, (hh + 1) * V_DIM)
        k = rope(zk[:, sl])
        kf_ref[head_rows(hh), :] = k
        if for_prompt:
            kb_ref[:, sl] = k.astype(BF16)
    zv = _dot(h, w_ref[:, 2 * QK_WIDTH:2 * QK_WIDTH + ATT_WIDTH])
    for hh in range(N_HEADS):
        vf_ref[head_rows(hh), :] = zv[:, hh * V_DIM:(hh + 1) * V_DIM]
    u = _dot(h, w_ref[:, 2 * QK_WIDTH + ATT_WIDTH:])
    if not for_prompt:
        u_ref[...] = u
    else:
        ones = jnp.ones((ONES_ROWS, KV_TILE), BF16)
        for c in range(tm // KV_TILE):
            vt = zv[c * KV_TILE:(c + 1) * KV_TILE, :].T.astype(BF16)
            for hh in range(N_HEADS):
                vt_ref[c, hh * VT_ROWS:hh * VT_ROWS + V_DIM, :] = vt[hh * V_DIM:(hh + 1) * V_DIM]
                vt_ref[c, hh * VT_ROWS + V_DIM:(hh + 1) * VT_ROWS, :] = ones
        _pool_rows(u, ext_ref, wp_ref, sc_ref, pool_ref, tail_ref, tm, tiles_per_seq)


def _pool_rows(u, ext_ref, w_ref, sc_ref, o_ref, tail_ref, tm, tiles_per_seq):
    it = pl.program_id(0) % tiles_per_seq

    @pl.when(it == 0)
    def _zero_history():
        ext_ref[0:HIST_PAD, :] = jnp.zeros((HIST_PAD, POOL_WIDTH), F32)

    @pl.when(it != 0)
    def _carry_history():
        ext_ref[0:HIST_PAD, :] = ext_ref[tm:tm + HIST_PAD, :]

    ext_ref[HIST_PAD:, :] = u
    tail_ref[...] = u[tm - HIST_PAD:, :]
    pos = it * tm + lax.broadcasted_iota(jnp.int32, (tm, POOL_GROUP), 0)
    for g, w in enumerate(POOL_WINDOWS):
        sl = slice(g * POOL_GROUP, (g + 1) * POOL_GROUP)
        cur = u[:, sl]
        tot = cur
        for j in range(1, w):
            tot = tot + ext_ref[HIST_PAD - j:HIST_PAD - j + tm, sl]
        cnt = jnp.minimum(pos + 1, w).astype(F32)
        pooled = (tot / cnt - cur).astype(BF16)
        o_ref[:, sl] = (_dot(pooled, w_ref[g]) * sc_ref[:, sl]).astype(o_ref.dtype)


def _layer_spec(a, layer, index_map_tail, **kw):
    return pl.BlockSpec((None,) + a.shape[1:], lambda *_: (layer,) + index_map_tail, **kw)


def _proj(x, g, w_in, tabs, layer, tm, tiles_per_seq, pool_w=None):
    m = x.shape[0]
    for_prompt = pool_w is not None
    row = lambda i: (i, 0)
    tab = lambda i: (i % tiles_per_seq, 0)
    wide = pl.BlockSpec((tm, ATT_WIDTH), row)
    head_major = pl.BlockSpec((tm * N_HEADS, V_DIM), row)
    rows_f32 = jax.ShapeDtypeStruct((m * N_HEADS, V_DIM), F32)
    in_specs = [
        pl.BlockSpec((tm, D_MODEL), row),
        _layer_spec(g, layer, (0, 0)),
        _layer_spec(w_in, layer, (0, 0)),
        pl.BlockSpec((tm, LANES), tab),
        pl.BlockSpec((tm, LANES), tab),
        pl.BlockSpec((tm, LANES), tab),
    ]
    args = [x, g, w_in, *tabs]
    out_specs = [wide, head_major, head_major]
    out_shape = [jax.ShapeDtypeStruct((m, QK_WIDTH), BF16), rows_f32, rows_f32]
    scratch = []
    if for_prompt:
        w_pool, scale = pool_w
        in_specs += [_layer_spec(w_pool, layer, (0, 0, 0)), _layer_spec(scale, layer, (0, 0))]
        args += [w_pool, scale]
        per = tm // KV_TILE
        out_specs += [wide,
                      pl.BlockSpec((per, N_HEADS * VT_ROWS, KV_TILE), lambda i: (i, 0, 0)),
                      wide,
                      pl.BlockSpec((HIST_PAD, POOL_WIDTH), lambda i: (i // tiles_per_seq, 0))]
        out_shape += [jax.ShapeDtypeStruct((m, QK_WIDTH), BF16),
                      jax.ShapeDtypeStruct((m // KV_TILE, N_HEADS * VT_ROWS, KV_TILE), BF16),
                      jax.ShapeDtypeStruct((m, POOL_WIDTH), BF16),
                      jax.ShapeDtypeStruct((m // tiles_per_seq // tm * HIST_PAD, POOL_WIDTH), F32)]
        scratch = [pltpu.VMEM((tm + HIST_PAD, POOL_WIDTH), F32)]
    else:
        out_specs += [wide]
        out_shape += [jax.ShapeDtypeStruct((m, POOL_WIDTH), F32)]
    return pl.pallas_call(
        functools.partial(_proj_kernel, tm=tm, tiles_per_seq=tiles_per_seq, for_prompt=for_prompt),
        grid=(m // tm,),
        in_specs=in_specs,
        out_specs=out_specs,
        out_shape=out_shape,
        scratch_shapes=scratch,
        compiler_params=pltpu.CompilerParams(
            dimension_semantics=("arbitrary",), vmem_limit_bytes=VMEM_LIMIT),
        name="proj",
    )(*args)


def _rope_tables(pos):
    lane = jnp.arange(LANES) % HEAD_DIM
    inv = jnp.power(ROPE_THETA, -(2 * (lane % ROT_HALF)).astype(F32) / ROT_DIM)
    ang = pos.astype(F32)[:, None] * inv[None, :]
    cos, sin = jnp.cos(ang), jnp.sin(ang)
    first = (lane < ROT_HALF)[None, :]
    second = ((lane >= ROT_HALF) & (lane < ROT_DIM))[None, :]
    ta = jnp.where(first | second, cos, 1.0)
    tb = jnp.where(first, -sin, 0.0)
    tc = jnp.where(second, sin, 0.0)
    return ta, tb, tc


def _attn_prompt_kernel(*refs, lam_init, tq, tk, nh, n_stack):
    lp_ref, gsub_ref, q_ref, k_ref, vt_ref = refs[:5]
    rows_hbm = refs[5:5 + n_stack]
    o_ref = refs[5 + n_stack]
    stacked_hbm = refs[6 + n_stack:6 + n_stack + (2 if n_stack else 0)]
    qst_ref, m_ref, acc_ref = refs[6 + n_stack + len(stacked_hbm):][:3]
    qi = pl.program_id(2)
    if n_stack:
        sem = refs[-1]
        depth = n_stack // 2
        chunk = rows_hbm[0].shape[0] // STACK_CHUNKS
        copies = []
        for t in range(2):
            for d in range(depth):
                for c in range(STACK_CHUNKS):
                    part = pl.ds(c * chunk, chunk)
                    copies.append(pltpu.make_async_copy(rows_hbm[t * depth + d].at[part],
                                                        stacked_hbm[t].at[d, part],
                                                        sem.at[t * depth + d]))
        step_id = (pl.program_id(0) * pl.num_programs(1) + pl.program_id(1)) * pl.num_programs(2) + qi
        n_steps = pl.num_programs(0) * pl.num_programs(1) * pl.num_programs(2)

        @pl.when(step_id == 0)
        def _start_stack():
            for c in copies:
                c.start()

    comp = lax.broadcasted_iota(jnp.int32, (V_DIM, tq), 0) < HEAD_DIM
    for hh in range(nh):
        qt = q_ref[:, hh * V_DIM:(hh + 1) * V_DIM].astype(F32).T
        qst_ref[hh, :, 0:tq] = jnp.where(comp, qt, 0.0).astype(BF16)
        qst_ref[hh, :, tq:] = jnp.where(comp, 0.0, qt).astype(BF16)
    m_ref[...] = jnp.full(m_ref.shape, NEG, F32)
    acc_ref[...] = jnp.zeros(acc_ref.shape, F32)

    def step(j, mask):
        start = pl.multiple_of(j * tk, tk)
        scores = []
        for hh in range(nh):
            s = _dot(k_ref[pl.ds(start, tk), hh * V_DIM:(hh + 1) * V_DIM], qst_ref[hh])
            scores.append(s if mask is None else jnp.where(mask, s, NEG))
        probs = []
        for hh in range(nh):
            m_old = m_ref[hh]
            m_new = jnp.maximum(m_old, jnp.max(scores[hh], axis=0, keepdims=True))
            m_ref[hh] = m_new
            probs.append((jnp.exp2(m_old - m_new), jnp.exp2(scores[hh] - m_new).astype(BF16)))
        for hh in range(nh):
            alpha, p = probs[hh]
            acc_ref[hh] = alpha * acc_ref[hh] + _dot(vt_ref[j, hh * VT_ROWS:(hh + 1) * VT_ROWS, :], p)

    per = tq // tk

    def body(j, carry):
        step(j, None)
        return carry

    lax.fori_loop(0, qi * per, body, 0)
    row = lax.broadcasted_iota(jnp.int32, (tk, 2 * tq), 0)
    col = lax.broadcasted_iota(jnp.int32, (tk, 2 * tq), 1)
    for d in range(per):
        step(qi * per + d, (d * tk + row) <= (col & (tq - 1)))

    lam = _lam(lp_ref, lam_init)
    for hh in range(nh):
        acc = acc_ref[hh, 0:V_DIM, :]
        l = acc_ref[hh, V_DIM:V_DIM + 1, :]
        ot = acc[:, 0:tq] / l[:, 0:tq] - lam * (acc[:, tq:] / l[:, tq:])
        o_ref[:, hh * V_DIM:(hh + 1) * V_DIM] = _subln(ot.T, gsub_ref[...], lam_init).astype(o_ref.dtype)

    if n_stack:
        @pl.when(step_id == n_steps - 1)
        def _wait_stack():
            for c in copies:
                c.wait()


def _attn_prompt(q, k, vt, lp, gsub, layer, lam_init, batch, seq, stack_rows=()):
    tq, tk, nh = Q_TILE, KV_TILE, HEADS_PER_STEP
    nq = seq // tq
    n_stack = len(stack_rows)
    kern = functools.partial(_attn_prompt_kernel, lam_init=lam_init, tq=tq, tk=tk, nh=nh,
                             n_stack=n_stack)
    q_spec = pl.BlockSpec((tq, nh * V_DIM), lambda b, h, i: (b * nq + i, h))
    hbm = pl.BlockSpec(memory_space=pl.ANY)
    out_specs = [q_spec]
    out_shape = [jax.ShapeDtypeStruct((batch * seq, ATT_WIDTH), BF16)]
    scratch = [
        pltpu.VMEM((nh, V_DIM, 2 * tq), BF16),
        pltpu.VMEM((nh, 1, 2 * tq), F32),
        pltpu.VMEM((nh, VT_ROWS, 2 * tq), F32),
    ]
    if n_stack:
        assert stack_rows[0].shape[0] % STACK_CHUNKS == 0
        stacked = jax.ShapeDtypeStruct((n_stack // 2,) + stack_rows[0].shape, stack_rows[0].dtype)
        out_specs += [hbm, hbm]
        out_shape += [stacked, stacked]
        scratch += [pltpu.SemaphoreType.DMA((n_stack,))]
    outs = pl.pallas_call(
        kern,
        grid=(batch, N_HEADS // nh, nq),
        in_specs=[
            _layer_spec(lp, layer, (0, 0)),
            _layer_spec(gsub, layer, (0, 0)),
            q_spec,
            pl.BlockSpec((seq, nh * V_DIM), lambda b, h, i: (b, h)),
            pl.BlockSpec((seq // tk, nh * VT_ROWS, tk), lambda b, h, i: (b, h, 0)),
        ] + [hbm] * n_stack,
        out_specs=out_specs,
        out_shape=out_shape,
        scratch_shapes=scratch,
        compiler_params=pltpu.CompilerParams(
            dimension_semantics=("arbitrary", "arbitrary", "arbitrary"),
            vmem_limit_bytes=VMEM_LIMIT),
        name="attn_prompt",
    )(lp, gsub, q, k, vt, *stack_rows)
    return outs if n_stack else outs[0]


def _pool_sample_kernel(hist_ref, u_ref, w_ref, sc_ref, o_ref, *, t_new, pos0):
    ext = [hist_ref[j] for j in range(POOL_HIST)] + [u_ref[t] for t in range(t_new)]
    for g, w in enumerate(POOL_WINDOWS):
        sl = slice(g * POOL_GROUP, (g + 1) * POOL_GROUP)
        rows = []
        for t in range(t_new):
            cur = ext[POOL_HIST + t][:, sl]
            tot = cur
            for j in range(1, w):
                tot = tot + ext[POOL_HIST + t - j][:, sl]
            cnt = float(min(pos0 + t + 1, w))
            rows.append(tot / cnt - cur)
        pooled = jnp.concatenate(rows, axis=0).astype(BF16)
        y = _dot(pooled, w_ref[g]) * sc_ref[:, sl]
        nb = y.shape[0] // t_new
        for t in range(t_new):
            o_ref[t, :, sl] = y[t * nb:(t + 1) * nb]


def _pool_sample(hist_tm, u_tm, w_pool, scale, layer, pos0):
    t_new, nb, _ = u_tm.shape
    kern = functools.partial(_pool_sample_kernel, t_new=t_new, pos0=pos0)
    whole = lambda a: pl.BlockSpec(a.shape, lambda i: (0,) * a.ndim)
    return pl.pallas_call(
        kern,
        grid=(1,),
        in_specs=[whole(hist_tm), whole(u_tm), _layer_spec(w_pool, layer, (0, 0, 0)),
                  _layer_spec(scale, layer, (0, 0))],
        out_specs=pl.BlockSpec((t_new, nb, POOL_WIDTH), lambda i: (0, 0, 0)),
        out_shape=jax.ShapeDtypeStruct((t_new, nb, POOL_WIDTH), F32),
        name="pool_sample",
    )(hist_tm, u_tm, w_pool, scale)


def _sample_attention(pt_ref, lp_ref, gsub_ref, w_ref, kn_ref, vn_ref, ck_hbm, cv_hbm, o_ref,
                      kbuf, vbuf, sem, *, layer, lam_init, t_new, n_seq, n_groups):
    g_pages = PAGES_PER_GROUP
    b = pl.program_id(0)
    w = w_ref[0]
    head_shift = (2 * t_new).bit_length() - 1
    tok_shift = N_HEADS.bit_length() - 1

    def own_head(shape):
        row = lax.broadcasted_iota(jnp.int32, shape, 0)
        col = lax.broadcasted_iota(jnp.int32, shape, 1)
        return (col & (N_HEADS - 1)) == (row >> head_shift), row, col

    def group_copies(seq, grp, slot):
        out = []
        for j in range(g_pages):
            page = pt_ref[seq, grp * g_pages + j]
            rows = pl.ds(j * PAGE_ROWS, PAGE_ROWS)
            out.append(pltpu.make_async_copy(ck_hbm.at[layer, page], kbuf.at[slot, rows], sem.at[slot, 0]))
            out.append(pltpu.make_async_copy(cv_hbm.at[layer, page], vbuf.at[slot, rows], sem.at[slot, 1]))
        return out

    ahead = GROUP_SLOTS - 1

    def first():
        @pl.when(b == 0)
        def _first():
            for g in range(ahead):
                for c in group_copies(0, g, g):
                    c.start()

    def init():
        sn = _dot_nt(w, kn_ref[0])
        own, row, col = own_head(sn.shape)
        valid = own & ((col >> tok_shift) <= (row & (t_new - 1)))
        sn = jnp.where(valid, sn, NEG)
        m0 = jnp.max(sn, axis=1, keepdims=True)
        p0 = jnp.where(valid, jnp.exp2(sn - m0), 0.0)
        return m0, jnp.sum(p0, axis=1, keepdims=True), _dot(p0.astype(BF16), vn_ref[0])

    def group(g, carry):
        m_old, l_old, acc_old = carry
        slot = g % GROUP_SLOTS
        nxt = g + ahead
        if nxt < n_groups:
            for c in group_copies(b, nxt, nxt % GROUP_SLOTS):
                c.start()
        else:
            @pl.when(b + 1 < n_seq)
            def _next_sequence():
                for c in group_copies(b + 1, nxt - n_groups, nxt % GROUP_SLOTS):
                    c.start()

        for c in group_copies(b, g, slot):
            c.wait()

        bias = jnp.where(own_head((w.shape[0], PAGE_ROWS))[0], 0.0, NEG)
        scores = []
        for j in range(g_pages):
            kj = kbuf[slot, j * PAGE_ROWS:(j + 1) * PAGE_ROWS, :].astype(BF16)
            scores.append(_dot_nt(w, kj) + bias)
        top = scores[0]
        for s in scores[1:]:
            top = jnp.maximum(top, s)
        m_new = jnp.maximum(m_old, jnp.max(top, axis=1, keepdims=True))
        alpha = jnp.exp2(m_old - m_new)
        tot = None
        acc = alpha * acc_old
        for j in range(g_pages):
            p = jnp.exp2(scores[j] - m_new)
            tot = p if tot is None else tot + p
            vj = vbuf[slot, j * PAGE_ROWS:(j + 1) * PAGE_ROWS, :].astype(BF16)
            acc = acc + _dot(p.astype(BF16), vj)
        l_new = alpha * l_old + jnp.sum(tot, axis=1, keepdims=True)
        return m_new, l_new, acc

    def finish(carry):
        _, l, acc = carry
        lam = _lam(lp_ref, lam_init)
        accn = acc / l
        per = 2 * t_new
        for hh in range(N_HEADS):
            blk = accn[hh * per:(hh + 1) * per]
            o = blk[0:t_new] - lam * blk[t_new:per]
            o_ref[0, :, hh * V_DIM:(hh + 1) * V_DIM] = _subln(o, gsub_ref[...], lam_init)

    return first, init, group, finish


N_FINISH_IN = 13


def _finish_kernel(*refs, sample):
    if sample is None:
        ins, o_ref = refs[:N_FINISH_IN], refs[N_FINISH_IN]
    else:
        ins = refs[1:1 + N_FINISH_IN]
        o_ref = refs[1 + N_FINISH_IN + 7]
    (x_ref, att_ref, pool_ref, p_ref, wo_ref, gpm_ref, gpf_ref, gqf_ref,
     wg_ref, wu_ref, wd_ref, wpg_ref, wpp_ref) = ins
    v = {}

    def mix_att():
        v["mix"] = _dot(att_ref[...].astype(BF16), wo_ref[0:ATT_WIDTH, :])

    def mix_pool():
        m = v["mix"] + _dot(pool_ref[...].astype(BF16), wo_ref[ATT_WIDTH:, :])
        v["x1"] = x_ref[...] + _rms(m, gpm_ref[...])
        v["h"] = _rms(v["x1"], gpf_ref[...]).astype(BF16)

    def ffn(sl):
        a = (jax.nn.silu(_dot(v["h"], wg_ref[:, sl])) * _dot(v["h"], wu_ref[:, sl])).astype(BF16)
        part = _dot(a, wd_ref[sl, :])
        v["f"] = part + v["f"] if "f" in v else part

    def ple(sl):
        if "x2" not in v:
            v["x2"] = v["x1"] + _rms(v["f"], gqf_ref[...])
            v["x2b"] = v["x2"].astype(BF16)
            v["pb"] = p_ref[...].astype(BF16)
        gate = jax.nn.sigmoid(_dot(v["x2b"], wpg_ref[:, sl]))
        o_ref[:, sl] = v["x2"][:, sl] + gate * _dot(v["pb"], wpp_ref[:, sl])

    chunks = [slice(c, min(c + FF_CHUNK, D_FF)) for c in range(0, D_FF, FF_CHUNK)]
    halves = [slice(0, D_MODEL // 2), slice(D_MODEL // 2, D_MODEL)]
    stages = ([mix_att, mix_pool] + [functools.partial(ffn, sl) for sl in chunks]
              + [functools.partial(ple, sl) for sl in halves])
    if sample is None:
        for stage in stages:
            stage()
        return

    pt_ref = refs[0]
    lp_ref, gsub_ref, wq_ref, kn_ref, vn_ref, ck_hbm, cv_hbm = refs[1 + N_FINISH_IN:1 + N_FINISH_IN + 7]
    os_ref, kbuf, vbuf, sem = refs[1 + N_FINISH_IN + 8:]
    n_groups = sample["n_groups"]
    first, init, group, finish = _sample_attention(
        pt_ref, lp_ref, gsub_ref, wq_ref, kn_ref, vn_ref, ck_hbm, cv_hbm, os_ref, kbuf, vbuf, sem,
        **sample)
    first()
    carry = init()
    done = 0
    for k, stage in enumerate(stages):
        upto = (k + 1) * n_groups // len(stages)
        for g in range(done, upto):
            carry = group(g, carry)
        done = upto
        stage()
    finish(carry)


def _finish(x, att, pool, p, w_o, g_post_mix, g_pre_ffn, g_post_ffn, w_gate, w_up, w_down,
            w_ple_gate, w_ple_proj, layer, tm, sample=None):
    m = x.shape[0]
    row = lambda i, *_: (i, 0)
    resident = lambda a: _layer_spec(a, layer, (0, 0), pipeline_mode=pl.Buffered(1))
    in_specs = [
        pl.BlockSpec((tm, D_MODEL), row),
        pl.BlockSpec((tm, ATT_WIDTH), row),
        pl.BlockSpec((tm, POOL_WIDTH), row),
        pl.BlockSpec((None, tm, PLE_DIM), lambda i, *_: (layer, i, 0)),
        resident(w_o), resident(g_post_mix), resident(g_pre_ffn), resident(g_post_ffn),
        resident(w_gate), resident(w_up), resident(w_down),
        resident(w_ple_gate), resident(w_ple_proj),
    ]
    args = [x, att, pool, p, w_o, g_post_mix, g_pre_ffn, g_post_ffn, w_gate, w_up, w_down,
            w_ple_gate, w_ple_proj]
    assert len(args) == N_FINISH_IN
    out_specs = [pl.BlockSpec((tm, D_MODEL), row)]
    out_shape = [jax.ShapeDtypeStruct((m, D_MODEL), F32)]
    scratch, prefetch, cfg = [], [], None
    if sample is not None:
        page_table, lp, gsub, w_blk, k_new, v_new, cache_k, cache_v, lam_init, t_new = sample
        nb, n_pages = page_table.shape
        n_groups = n_pages // PAGES_PER_GROUP
        assert nb == m // tm and n_pages % PAGES_PER_GROUP == 0 and n_groups % GROUP_SLOTS == 0
        per_seq = lambda i, *_: (i, 0, 0)
        in_specs += [
            _layer_spec(lp, layer, (0, 0)),
            _layer_spec(gsub, layer, (0, 0)),
            pl.BlockSpec((1,) + w_blk.shape[1:], per_seq),
            pl.BlockSpec((1,) + k_new.shape[1:], per_seq),
            pl.BlockSpec((1,) + v_new.shape[1:], per_seq),
            pl.BlockSpec(memory_space=pl.ANY),
            pl.BlockSpec(memory_space=pl.ANY),
        ]
        args += [lp, gsub, w_blk, k_new, v_new, cache_k, cache_v]
        out_specs += [pl.BlockSpec((1, t_new, ATT_WIDTH), per_seq)]
        out_shape += [jax.ShapeDtypeStruct((nb, t_new, ATT_WIDTH), F32)]
        scratch = [
            pltpu.VMEM((GROUP_SLOTS, PAGES_PER_GROUP * PAGE_ROWS, V_DIM), F32),
            pltpu.VMEM((GROUP_SLOTS, PAGES_PER_GROUP * PAGE_ROWS, V_DIM), F32),
            pltpu.SemaphoreType.DMA((GROUP_SLOTS, 2)),
        ]
        prefetch = [page_table]
        cfg = dict(layer=layer, lam_init=lam_init, t_new=t_new, n_seq=nb, n_groups=n_groups)
    grid_spec = pltpu.PrefetchScalarGridSpec(
        num_scalar_prefetch=len(prefetch), grid=(m // tm,), in_specs=in_specs,
        out_specs=out_specs, scratch_shapes=scratch)
    outs = pl.pallas_call(
        functools.partial(_finish_kernel, sample=cfg),
        grid_spec=grid_spec,
        out_shape=out_shape,
        compiler_params=pltpu.CompilerParams(
            dimension_semantics=("arbitrary",), vmem_limit_bytes=VMEM_LIMIT),
        name="finish",
    )(*prefetch, *args)
    return outs[0] if sample is None else outs


def _query_rows(q, nb, t_new):
    qr = q.reshape(nb, t_new, N_HEADS, 2, HEAD_DIM).transpose(0, 2, 3, 1, 4)
    eye = jnp.eye(2, dtype=q.dtype)
    w = qr[:, :, :, :, None, :] * eye[None, None, :, None, :, None]
    return w.reshape(nb, N_HEADS * 2 * t_new, V_DIM)


def kernel(x_prompt, x_sample, p_prompt, p_sample, cache_k, cache_v, state_pool, page_table, w_in, lambda_params, g_sub, w_pool, pool_scale, w_o, g_pre_mix, g_post_mix, g_pre_ffn, g_post_ffn, w_gate, w_up, w_down, w_ple_gate, w_ple_proj):
    batch, seq, _ = x_prompt.shape
    nb, t_new, _ = x_sample.shape
    depth = w_in.shape[0]
    n_pages = page_table.shape[1]
    past_len = n_pages * PAGE_SIZE
    assert seq % ROW_TILE == 0 and seq % Q_TILE == 0
    assert ROW_TILE % KV_TILE == 0 and Q_TILE % KV_TILE == 0 and N_HEADS % HEADS_PER_STEP == 0
    assert t_new & (t_new - 1) == 0 and N_HEADS & (N_HEADS - 1) == 0

    tabs_p = _rope_tables(jnp.arange(seq))
    pos_s = past_len + jnp.arange(t_new)
    tabs_s = tuple(jnp.tile(a, (nb, 1)) for a in _rope_tables(pos_s))
    ck = cache_k.reshape(cache_k.shape[0], cache_k.shape[1], PAGE_ROWS, V_DIM)
    cv = cache_v.reshape(cache_v.shape[0], cache_v.shape[1], PAGE_ROWS, V_DIM)
    bf = lambda a: a.astype(BF16)
    rows = lambda a: a.reshape(depth, 1, -1)
    w_in_b, w_pool_b = bf(w_in), bf(w_pool)
    ffn_w = (bf(w_o), rows(g_post_mix), rows(g_pre_ffn), rows(g_post_ffn),
             bf(w_gate), bf(w_up), bf(w_down), bf(w_ple_gate), bf(w_ple_proj))
    g_in, gsub, scale = rows(g_pre_mix), rows(g_sub), rows(pool_scale)
    pp = p_prompt.reshape(depth, batch * seq, PLE_DIM)
    psm = p_sample.reshape(depth, nb * t_new, PLE_DIM)

    xp = x_prompt.reshape(batch * seq, D_MODEL)
    xs = x_sample.reshape(nb * t_new, D_MODEL)
    kp_rows, vp_rows, pp_rows, ks_rows, vs_rows, ps_rows = [], [], [], [], [], []
    for i in range(depth):
        lam_init = 0.8 - 0.6 * math.exp(-0.3 * i)

        q, kf, vf, kb, vt, pool, tail = _proj(xp, g_in, w_in_b, tabs_p, i, ROW_TILE,
                                              seq // ROW_TILE, (w_pool_b, scale))
        kp_rows.append(kf)
        vp_rows.append(vf)
        if i + 1 < depth:
            att = _attn_prompt(q, kb, vt, lambda_params, gsub, i, lam_init, batch, seq)
        else:
            att, k_prompt, v_prompt = _attn_prompt(q, kb, vt, lambda_params, gsub, i, lam_init,
                                                   batch, seq, stack_rows=(*kp_rows, *vp_rows))
        pp_rows.append(tail.reshape(batch, HIST_PAD, POOL_WIDTH)[:, HIST_PAD - POOL_HIST:])

        q, kf, vf, u = _proj(xs, g_in, w_in_b, tabs_s, i, nb * t_new, 1)
        new_rows = lambda a: bf(a.reshape(nb, t_new * N_HEADS, V_DIM))
        xp, att = _finish(xp, att, pool, pp, *ffn_w, i, ROW_TILE,
                          sample=(page_table, lambda_params, gsub, _query_rows(q, nb, t_new),
                                  new_rows(kf), new_rows(vf), ck, cv, lam_init, t_new))
        u3 = u.reshape(nb, t_new, POOL_WIDTH)
        pool = _pool_sample(state_pool[i].transpose(1, 0, 2), u3.transpose(1, 0, 2), w_pool_b,
                            scale, i, past_len)
        pool = pool.transpose(1, 0, 2).reshape(nb * t_new, POOL_WIDTH)
        xs = _finish(xs, att.reshape(nb * t_new, ATT_WIDTH), pool, psm, *ffn_w, i, nb * t_new)
        ks_rows.append(kf.reshape(nb, t_new, N_HEADS, V_DIM))
        vs_rows.append(vf.reshape(nb, t_new, N_HEADS, V_DIM))
        ps_rows.append(jnp.concatenate([state_pool[i], u3], axis=1)[:, t_new:])

    return (xp.reshape(batch, seq, D_MODEL), xs.reshape(nb, t_new, D_MODEL),
            k_prompt.reshape(depth, batch, seq, N_HEADS, V_DIM),
            v_prompt.reshape(depth, batch, seq, N_HEADS, V_DIM), jnp.stack(pp_rows),
            jnp.stack(ks_rows), jnp.stack(vs_rows), jnp.stack(ps_rows))
```
